```python
import jax, jax.numpy as jnp
from jax import lax
import numpy as np

D_MODEL = 1024
BATCH = 8
SEQ = 8192
DEPTH = 2

PLE_DIM = 256
D_FF = 2816
D_POOL = D_MODEL
N_POOL_GROUPS = 4
POOL_GROUP = D_POOL // N_POOL_GROUPS
POOL_WINDOWS = (2, 4, 8, 16)
D_CONV = D_MODEL
CONV_K = 31
N_IN = D_POOL + 2 * D_CONV + 2 * D_MODEL
RMS_EPS = 1e-6
LN_EPS = 1e-5

kernel_name = "hybrid_pool_conformer_macaron_ple"


def rmsnorm(x, g):
    x32 = x.astype(jnp.float32)
    y = x32 * lax.rsqrt(jnp.mean(x32 * x32, axis=-1, keepdims=True) + RMS_EPS)
    return (y * g.astype(jnp.float32)).astype(x.dtype)


def layernorm(x, g, b):
    x32 = x.astype(jnp.float32)
    mu = jnp.mean(x32, axis=-1, keepdims=True)
    var = jnp.mean(jnp.square(x32 - mu), axis=-1, keepdims=True)
    y = (x32 - mu) * lax.rsqrt(var + LN_EPS)
    return (y * g.astype(jnp.float32) + b.astype(jnp.float32)).astype(x.dtype)


def swiglu(x, w_gate, w_up, w_down):
    return (jax.nn.silu(x @ w_gate) * (x @ w_up)) @ w_down


def causal_multiscale_pool(z):
    S = z.shape[1]
    cs = jnp.cumsum(z.astype(jnp.float32), axis=1)
    pos = jnp.arange(S, dtype=jnp.int32)
    outs = []
    for g, w in enumerate(POOL_WINDOWS):
        sl = slice(g * POOL_GROUP, (g + 1) * POOL_GROUP)
        cs_g = cs[..., sl]
        lower = jnp.pad(cs_g, ((0, 0), (w, 0), (0, 0)))[:, :S]
        count = jnp.minimum(pos + 1, w).astype(jnp.float32)[None, :, None]
        mean = (cs_g - lower) / count
        outs.append(mean - z[..., sl].astype(jnp.float32))
    return jnp.concatenate(outs, axis=-1).astype(z.dtype)


def causal_depthwise_conv(x, w, b):
    K, C = w.shape
    y = lax.conv_general_dilated(
        x, w[:, None, :].astype(x.dtype), window_strides=(1,), padding=[(K - 1, 0)],
        dimension_numbers=("NWC", "WIO", "NWC"), feature_group_count=C)
    return y + b


def setup_inputs(seed: int = 0) -> dict:
    key = jax.random.key(seed)
    ks = iter(jax.random.split(key, 32))

    def nrm(shape, fan_in):
        return jax.random.normal(next(ks), shape, jnp.float32) * (fan_in ** -0.5)

    def gain(shape):
        return 1.0 + 0.02 * jax.random.normal(next(ks), shape, jnp.float32)

    def bias(shape):
        return 0.02 * jax.random.normal(next(ks), shape, jnp.float32)

    L = DEPTH
    return {
        "x": jax.random.normal(next(ks), (BATCH, SEQ, D_MODEL), jnp.float32),
        "p": jax.random.normal(next(ks), (DEPTH, BATCH, SEQ, PLE_DIM), jnp.float32),
        "ffn1_norm": gain((L, D_MODEL)),
        "ffn1_w_gate": nrm((L, D_MODEL, D_FF), D_MODEL),
        "ffn1_w_up": nrm((L, D_MODEL, D_FF), D_MODEL),
        "ffn1_w_down": nrm((L, D_FF, D_MODEL), D_FF),
        "mix_norm": gain((L, D_MODEL)),
        "w_in": nrm((L, D_MODEL, N_IN), D_MODEL),
        "pool_w": nrm((L, N_POOL_GROUPS, POOL_GROUP, POOL_GROUP), POOL_GROUP),
        "pool_scale": gain((L, D_POOL)),
        "conv_dw_w": nrm((L, CONV_K, D_CONV), CONV_K),
        "conv_dw_b": bias((L, D_CONV)),
        "conv_ln_g": gain((L, D_CONV)),
        "conv_ln_b": bias((L, D_CONV)),
        "conv_w_out": nrm((L, D_CONV, D_MODEL), D_CONV),
        "w_out": nrm((L, D_MODEL, D_MODEL), D_MODEL),
        "ffn2_norm": gain((L, D_MODEL)),
        "ffn2_w_gate": nrm((L, D_MODEL, D_FF), D_MODEL),
        "ffn2_w_up": nrm((L, D_MODEL, D_FF), D_MODEL),
        "ffn2_w_down": nrm((L, D_FF, D_MODEL), D_FF),
        "ple_norm": gain((L, D_MODEL)),
        "ple_w_gate": nrm((L, D_MODEL, D_MODEL), D_MODEL),
        "ple_w_proj": nrm((L, PLE_DIM, D_MODEL), PLE_DIM),
        "final_norm": gain((D_MODEL,)),
    }


def reference(x, p, ffn1_norm, ffn1_w_gate, ffn1_w_up, ffn1_w_down, mix_norm, w_in,
              pool_w, pool_scale, conv_dw_w, conv_dw_b, conv_ln_g, conv_ln_b, conv_w_out,
              w_out, ffn2_norm, ffn2_w_gate, ffn2_w_up, ffn2_w_down, ple_norm, ple_w_gate,
              ple_w_proj, final_norm):
    B, S, _ = x.shape
    h = x
    split_pts = [D_POOL, D_POOL + D_CONV, D_POOL + 2 * D_CONV, D_POOL + 2 * D_CONV + D_MODEL]
    for i in range(DEPTH):
        h = h + 0.5 * swiglu(rmsnorm(h, ffn1_norm[i]), ffn1_w_gate[i], ffn1_w_up[i], ffn1_w_down[i])

        u = rmsnorm(h, mix_norm[i])
        z = u @ w_in[i]
        z_pool, z_glu_a, z_glu_g, g_pool, g_conv = jnp.split(z, split_pts, axis=-1)

        pooled = causal_multiscale_pool(z_pool).reshape(B, S, N_POOL_GROUPS, POOL_GROUP)
        a = jnp.einsum("bsgc,gcd->bsgd", pooled, pool_w[i]).reshape(B, S, D_POOL)
        a = a * pool_scale[i]

        c = z_glu_a * jax.nn.sigmoid(z_glu_g)
        c = causal_depthwise_conv(c, conv_dw_w[i], conv_dw_b[i])
        c = jax.nn.silu(layernorm(c, conv_ln_g[i], conv_ln_b[i]))
        c = c @ conv_w_out[i]

        m = jax.nn.sigmoid(g_pool) * a + jax.nn.sigmoid(g_conv) * c
        h = h + m @ w_out[i]

        h = h + 0.5 * swiglu(rmsnorm(h, ffn2_norm[i]), ffn2_w_gate[i], ffn2_w_up[i], ffn2_w_down[i])

        gate = jax.nn.sigmoid(rmsnorm(h, ple_norm[i]) @ ple_w_gate[i])
        h = h + gate * (p[i] @ ple_w_proj[i])
    return rmsnorm(h, final_norm)
```

```python
import functools

import jax
import jax.numpy as jnp
from jax import lax
from jax.experimental import pallas as pl
from jax.experimental.pallas import tpu as pltpu

D_MODEL = 1024
PLE_DIM = 256
D_FF = 2816
N_POOL_GROUPS = 4
POOL_GROUP = D_MODEL // N_POOL_GROUPS
POOL_WINDOWS = (2, 4, 8, 16)
CONV_K = 31
RMS_EPS = 1e-6
LN_EPS = 1e-5

F32 = jnp.float32
BF16 = jnp.bfloat16

FFN_TM = 512
MIX_TM = 512
FF_CHUNKS = ((0, 512), (512, 512), (1024, 512), (1536, 512), (2048, 512), (2560, 256))
POOL_HALO = 16
CONV_HALO = 32
CONV_ROWS = 32
VMEM_LIMIT = 56 * 1024 * 1024


def _dot(a, b):
    return jnp.dot(a, b, preferred_element_type=F32)


def _rms(x, g):
    ms = jnp.mean(x * x, axis=-1, keepdims=True)
    return x * lax.rsqrt(ms + RMS_EPS) * g


def _const_spec(shape):
    nd = len(shape)
    return pl.BlockSpec(shape, lambda *_: (0,) * nd, pipeline_mode=pl.Buffered(1))


def _ffn_kernel(*refs, ple, final):
    h_ref, g_ref, wg_ref, wu_ref, wd_ref = refs[:5]
    o_ref = refs[-1]
    h = h_ref[...]
    xn = _rms(h, g_ref[...]).astype(BF16)
    acc = None
    for c0, cw in FF_CHUNKS:
        g = _dot(xn, wg_ref[:, c0:c0 + cw])
        u = _dot(xn, wu_ref[:, c0:c0 + cw])
        a = (g * jax.nn.sigmoid(g) * u).astype(BF16)
        d = _dot(a, wd_ref[c0:c0 + cw, :])
        acc = d if acc is None else acc + d
    out = h + 0.5 * acc
    if ple:
        p_ref, pn_ref, pwg_ref, pwp_ref = refs[5:9]
        hn = _rms(out, pn_ref[...]).astype(BF16)
        gate = jax.nn.sigmoid(_dot(hn, pwg_ref[...]))
        pe = _dot(p_ref[...].astype(BF16), pwp_ref[...])
        out = out + gate * pe
    if final:
        fn_ref = refs[9]
        out = _rms(out, fn_ref[...])
    o_ref[...] = out


def _ffn_call(h, norm, wg, wu, wd, ple_args=None, final_norm=None):
    n_tok = h.shape[0]
    tm = FFN_TM
    tok_spec = pl.BlockSpec((tm, D_MODEL), lambda i: (i, 0))
    in_specs = [tok_spec, _const_spec((1, D_MODEL)), _const_spec((D_MODEL, D_FF)),
                _const_spec((D_MODEL, D_FF)), _const_spec((D_FF, D_MODEL))]
    args = [h, norm, wg, wu, wd]
    if ple_args is not None:
        p, pn, pwg, pwp = ple_args
        in_specs += [pl.BlockSpec((tm, PLE_DIM), lambda i: (i, 0)), _const_spec((1, D_MODEL)),
                     _const_spec((D_MODEL, D_MODEL)), _const_spec((PLE_DIM, D_MODEL))]
        args += [p, pn, pwg, pwp]
    if final_norm is not None:
        in_specs.append(_const_spec((1, D_MODEL)))
        args.append(final_norm)
    return pl.pallas_call(
        functools.partial(_ffn_kernel, ple=ple_args is not None, final=final_norm is not None),
        grid=(n_tok // tm,),
        in_specs=in_specs,
        out_specs=tok_spec,
        out_shape=jax.ShapeDtypeStruct((n_tok, D_MODEL), F32),
        compiler_params=pltpu.CompilerParams(
            dimension_semantics=("arbitrary",), vmem_limit_bytes=VMEM_LIMIT),
        name="ffn_ple" if ple_args is not None else "ffn",
    )(*args)


def _mix_kernel(h_ref, nrm_ref, win_ref, pw_ref, ps_ref, cw_ref, cb_ref, lg_ref, lb_ref,
                cwo_ref, wo_ref, o_ref, zp_buf, c_buf, y_buf):
    tm = h_ref.shape[0]
    s = pl.program_id(1)

    @pl.when(s == 0)
    def _():
        zp_buf[0:POOL_HALO, :] = jnp.zeros((POOL_HALO, D_MODEL), F32)
        c_buf[0:CONV_HALO, :] = jnp.zeros((CONV_HALO, D_MODEL), F32)

    h = h_ref[...]
    u = _rms(h, nrm_ref[...]).astype(BF16)

    def zcol(i):
        return _dot(u, win_ref[:, i * D_MODEL:(i + 1) * D_MODEL])

    zp_buf[POOL_HALO:POOL_HALO + tm, :] = zcol(0)
    c_buf[CONV_HALO:CONV_HALO + tm, :] = zcol(1) * jax.nn.sigmoid(zcol(2))

    pos = s * tm + lax.broadcasted_iota(jnp.int32, (tm, POOL_GROUP), 0)
    a_parts = []
    for g, w in enumerate(POOL_WINDOWS):
        cols = slice(g * POOL_GROUP, (g + 1) * POOL_GROUP)
        z_g = zp_buf[POOL_HALO:POOL_HALO + tm, cols]
        ssum = z_g
        for j in range(1, w):
            ssum = ssum + zp_buf[POOL_HALO - j:POOL_HALO - j + tm, cols]
        count = jnp.minimum(pos + 1, w).astype(F32)
        pooled = ssum / count - z_g
        a_parts.append(_dot(pooled.astype(BF16), pw_ref[g]))
    a = jnp.concatenate(a_parts, axis=-1) * ps_ref[...]

    off = CONV_HALO - (CONV_K - 1)
    for r0 in range(0, tm, CONV_ROWS):
        acc = None
        for k in range(CONV_K):
            t = cw_ref[k:k + 1, :] * c_buf[r0 + off + k:r0 + off + k + CONV_ROWS, :]
            acc = t if acc is None else acc + t
        y_buf[r0:r0 + CONV_ROWS, :] = acc + cb_ref[...]
    y = y_buf[...]
    mu = jnp.mean(y, axis=-1, keepdims=True)
    yc = y - mu
    var = jnp.mean(yc * yc, axis=-1, keepdims=True)
    yn = yc * lax.rsqrt(var + LN_EPS) * lg_ref[...] + lb_ref[...]
    cm = _dot((yn * jax.nn.sigmoid(yn)).astype(BF16), cwo_ref[...])

    m = jax.nn.sigmoid(zcol(3)) * a + jax.nn.sigmoid(zcol(4)) * cm
    o_ref[...] = h + _dot(m.astype(BF16), wo_ref[...])

    zp_buf[0:POOL_HALO, :] = zp_buf[tm:tm + POOL_HALO, :]
    c_buf[0:CONV_HALO, :] = c_buf[tm:tm + CONV_HALO, :]


def _mix_call(h, nrm, win, pw, ps, cw, cb, lg, lb, cwo, wo):
    bsz, seq, _ = h.shape
    tm = MIX_TM
    tok_spec = pl.BlockSpec((None, tm, D_MODEL), lambda b, s: (b, s, 0))
    n_in = win.shape[1]
    in_specs = [tok_spec, _const_spec((1, D_MODEL)), _const_spec((D_MODEL, n_in)),
                _const_spec((N_POOL_GROUPS, POOL_GROUP, POOL_GROUP)), _const_spec((1, D_MODEL)),
                _const_spec((CONV_K, D_MODEL)), _const_spec((1, D_MODEL)),
                _const_spec((1, D_MODEL)), _const_spec((1, D_MODEL)),
                _const_spec((D_MODEL, D_MODEL)), _const_spec((D_MODEL, D_MODEL))]
    return pl.pallas_call(
        _mix_kernel,
        grid=(bsz, seq // tm),
        in_specs=in_specs,
        out_specs=tok_spec,
        out_shape=jax.ShapeDtypeStruct(h.shape, F32),
        scratch_shapes=[pltpu.VMEM((POOL_HALO + tm, D_MODEL), F32),
                        pltpu.VMEM((CONV_HALO + tm, D_MODEL), F32),
                        pltpu.VMEM((tm, D_MODEL), F32)],
        compiler_params=pltpu.CompilerParams(
            dimension_semantics=("arbitrary", "arbitrary"), vmem_limit_bytes=VMEM_LIMIT),
        name="mixer",
    )(h, nrm, win, pw, ps, cw, cb, lg, lb, cwo, wo)


def kernel(x, p, ffn1_norm, ffn1_w_gate, ffn1_w_up, ffn1_w_down, mix_norm, w_in, pool_w, pool_scale, conv_dw_w, conv_dw_b, conv_ln_g, conv_ln_b, conv_w_out, w_out, ffn2_norm, ffn2_w_gate, ffn2_w_up, ffn2_w_down, ple_norm, ple_w_gate, ple_w_proj, final_norm):
    bsz, seq, d = x.shape
    depth = p.shape[0]
    n_tok = bsz * seq
    row = lambda v: v.reshape(1, -1)
    bf = lambda w: w.astype(BF16)
    h = x
    for i in range(depth):
        h = _ffn_call(h.reshape(n_tok, d), row(ffn1_norm[i]), bf(ffn1_w_gate[i]),
                      bf(ffn1_w_up[i]), bf(ffn1_w_down[i]))
        h = _mix_call(h.reshape(bsz, seq, d), row(mix_norm[i]), bf(w_in[i]), bf(pool_w[i]),
                      row(pool_scale[i]), conv_dw_w[i], row(conv_dw_b[i]), row(conv_ln_g[i]),
                      row(conv_ln_b[i]), bf(conv_w_out[i]), bf(w_out[i]))
        ple_args = (p[i].reshape(n_tok, PLE_DIM), row(ple_norm[i]), bf(ple_w_gate[i]),
                    bf(ple_w_proj[i]))
        h = _ffn_call(h.reshape(n_tok, d), row(ffn2_norm[i]), bf(ffn2_w_gate[i]),
                      bf(ffn2_w_up[i]), bf(ffn2_w_down[i]), ple_args=ple_args,
                      final_norm=row(final_norm) if i == depth - 1 else None)
    return h.reshape(bsz, seq, d)
```

```python
import functools

import jax
import jax.numpy as jnp
from jax import lax
from jax.experimental import pallas as pl
from jax.experimental.pallas import tpu as pltpu

D_MODEL = 1024
PLE_DIM = 256
D_FF = 2816
N_POOL_GROUPS = 4
POOL_GROUP = D_MODEL // N_POOL_GROUPS
POOL_WINDOWS = (2, 4, 8, 16)
CONV_K = 31
RMS_EPS = 1e-6
LN_EPS = 1e-5

F32 = jnp.float32
BF16 = jnp.bfloat16

SUBLANES = 8
LANES = 128
TILE = 512
SEG = TILE // SUBLANES
FF_CHUNKS = ((0, 512), (512, 512), (1024, 512), (1536, 512), (2048, 512), (2560, 256))
POOL_HALO = 16
CONV_HALO = 32
CONV_ROWS = 8
VMEM_LIMIT = 56 * 1024 * 1024


def _dot(a, b):
    return jnp.dot(a, b, preferred_element_type=F32)


def _rms(x, g):
    ms = jnp.mean(x * x, axis=-1, keepdims=True)
    return x * lax.rsqrt(ms + RMS_EPS) * g


def _const_spec(shape):
    nd = len(shape)
    return pl.BlockSpec(shape, lambda *_: (0,) * nd, pipeline_mode=pl.Buffered(1))


def _ffn_kernel(*refs, ple, final):
    h_ref, g_ref, wg_ref, wu_ref, wd_ref = refs[:5]
    o_ref = refs[-1]
    h = h_ref[...]
    xn = _rms(h, g_ref[...]).astype(BF16)
    acc = None
    for c0, cw in FF_CHUNKS:
        g = _dot(xn, wg_ref[:, c0:c0 + cw])
        u = _dot(xn, wu_ref[:, c0:c0 + cw])
        a = (g * jax.nn.sigmoid(g) * u).astype(BF16)
        d = _dot(a, wd_ref[c0:c0 + cw, :])
        acc = d if acc is None else acc + d
    out = h + 0.5 * acc
    if ple:
        p_ref, pn_ref, pwg_ref, pwp_ref = refs[5:9]
        hn = _rms(out, pn_ref[...]).astype(BF16)
        gate = jax.nn.sigmoid(_dot(hn, pwg_ref[...]))
        pe = _dot(p_ref[...].astype(BF16), pwp_ref[...])
        out = out + gate * pe
    if final:
        fn_ref = refs[9]
        out = _rms(out, fn_ref[...])
    o_ref[...] = out


def _ffn_call(h, norm, wg, wu, wd, ple_args=None, final_norm=None):
    n_tok = h.shape[0]
    tok_spec = pl.BlockSpec((TILE, D_MODEL), lambda i: (i, 0))
    in_specs = [tok_spec, _const_spec((1, D_MODEL)), _const_spec((D_MODEL, D_FF)),
                _const_spec((D_MODEL, D_FF)), _const_spec((D_FF, D_MODEL))]
    args = [h, norm, wg, wu, wd]
    if ple_args is not None:
        p, pn, pwg, pwp = ple_args
        in_specs += [pl.BlockSpec((TILE, PLE_DIM), lambda i: (i, 0)), _const_spec((1, D_MODEL)),
                     _const_spec((D_MODEL, D_MODEL)), _const_spec((PLE_DIM, D_MODEL))]
        args += [p, pn, pwg, pwp]
    if final_norm is not None:
        in_specs.append(_const_spec((1, D_MODEL)))
        args.append(final_norm)
    return pl.pallas_call(
        functools.partial(_ffn_kernel, ple=ple_args is not None, final=final_norm is not None),
        grid=(n_tok // TILE,),
        in_specs=in_specs,
        out_specs=tok_spec,
        out_shape=jax.ShapeDtypeStruct((n_tok, D_MODEL), F32),
        compiler_params=pltpu.CompilerParams(
            dimension_semantics=("arbitrary",), vmem_limit_bytes=VMEM_LIMIT),
        name="ffn_ple" if ple_args is not None else "ffn",
    )(*args)


def _with_history(cur, carry_ref, halo):
    rolled = pltpu.roll(cur[SEG - halo:], 1, axis=1)
    sub = lax.broadcasted_iota(jnp.int32, rolled.shape, 1)
    hist = jnp.where(sub == 0, carry_ref[...], rolled)
    carry_ref[...] = rolled
    return hist


def _mix_kernel(h_ref, nrm_ref, win_ref, pw_ref, ps_ref, cw_ref, cb_ref, lg_ref, lb_ref,
                cwo_ref, wo_ref, o_ref, zp_buf, c_buf, y_buf, zp_carry, c_carry, w_rows):
    s = pl.program_id(1)

    @pl.when(s == 0)
    def _():
        zp_carry[...] = jnp.zeros(zp_carry.shape, F32)
        c_carry[...] = jnp.zeros(c_carry.shape, F32)
        for k in range(CONV_K):
            w_rows[k] = jnp.broadcast_to(cw_ref[k:k + 1, :], (SUBLANES, D_MODEL))

    h = h_ref[...]
    u = _rms(h, nrm_ref[...]).astype(BF16)

    def zcol(i):
        return _dot(u, win_ref[:, i * D_MODEL:(i + 1) * D_MODEL])

    zp = zcol(0).reshape(SEG, SUBLANES, D_MODEL)
    zp_buf[0:POOL_HALO] = _with_history(zp, zp_carry, POOL_HALO)
    zp_buf[POOL_HALO:] = zp
    c = (zcol(1) * jax.nn.sigmoid(zcol(2))).reshape(SEG, SUBLANES, D_MODEL)
    c_buf[0:CONV_HALO] = _with_history(c, c_carry, CONV_HALO)
    c_buf[CONV_HALO:] = c

    shp = (SEG, SUBLANES, POOL_GROUP)
    pos = (s * TILE + lax.broadcasted_iota(jnp.int32, shp, 1) * SEG
           + lax.broadcasted_iota(jnp.int32, shp, 0))
    a_parts = []
    for g, w in enumerate(POOL_WINDOWS):
        zg = zp_buf[:, :, g * POOL_GROUP:(g + 1) * POOL_GROUP]
        ssum, span = zg, 1
        while span < w:
            ssum = ssum[span:] + ssum[:-span]
            span *= 2
        ssum = ssum[POOL_HALO - (w - 1):]
        count = jnp.minimum(pos + 1, w).astype(F32)
        pooled = ssum / count - zg[POOL_HALO:]
        a_parts.append(_dot(pooled.reshape(TILE, POOL_GROUP).astype(BF16), pw_ref[g]))
    a = jnp.concatenate(a_parts, axis=-1) * ps_ref[...]

    off = CONV_HALO - (CONV_K - 1)

    def conv_rows(i, carry):
        r0 = i * CONV_ROWS
        for l in range(D_MODEL // LANES):
            lanes = slice(l * LANES, (l + 1) * LANES)
            acc = None
            for k in range(CONV_K):
                t = c_buf[pl.ds(r0 + off + k, CONV_ROWS), :, lanes] * w_rows[k, :, lanes][None]
                acc = t if acc is None else acc + t
            y_buf[pl.ds(r0, CONV_ROWS), :, lanes] = acc
        return carry

    lax.fori_loop(0, SEG // CONV_ROWS, conv_rows, 0)
    y = y_buf[...].reshape(TILE, D_MODEL) + cb_ref[...]
    mu = jnp.mean(y, axis=-1, keepdims=True)
    yc = y - mu
    var = jnp.mean(yc * yc, axis=-1, keepdims=True)
    yn = yc * lax.rsqrt(var + LN_EPS) * lg_ref[...] + lb_ref[...]
    cm = _dot((yn * jax.nn.sigmoid(yn)).astype(BF16), cwo_ref[...])

    m = jax.nn.sigmoid(zcol(3)) * a + jax.nn.sigmoid(zcol(4)) * cm
    o_ref[...] = h + _dot(m.astype(BF16), wo_ref[...])


def _mix_call(h, nrm, win, pw, ps, cw, cb, lg, lb, cwo, wo):
    bsz, seq, _ = h.shape
    tok_spec = pl.BlockSpec((None, TILE, D_MODEL), lambda b, s: (b, s, 0))
    n_in = win.shape[1]
    in_specs = [tok_spec, _const_spec((1, D_MODEL)), _const_spec((D_MODEL, n_in)),
                _const_spec((N_POOL_GROUPS, POOL_GROUP, POOL_GROUP)), _const_spec((1, D_MODEL)),
                _const_spec((CONV_K, D_MODEL)), _const_spec((1, D_MODEL)),
                _const_spec((1, D_MODEL)), _const_spec((1, D_MODEL)),
                _const_spec((D_MODEL, D_MODEL)), _const_spec((D_MODEL, D_MODEL))]
    vreg_rows = lambda n: pltpu.VMEM((n, SUBLANES, D_MODEL), F32)
    return pl.pallas_call(
        _mix_kernel,
        grid=(bsz, seq // TILE),
        in_specs=in_specs,
        out_specs=tok_spec,
        out_shape=jax.ShapeDtypeStruct(h.shape, F32),
        scratch_shapes=[vreg_rows(POOL_HALO + SEG), vreg_rows(CONV_HALO + SEG), vreg_rows(SEG),
                        vreg_rows(POOL_HALO), vreg_rows(CONV_HALO), vreg_rows(CONV_K)],
        compiler_params=pltpu.CompilerParams(
            dimension_semantics=("arbitrary", "arbitrary"), vmem_limit_bytes=VMEM_LIMIT),
        name="mixer",
    )(h, nrm, win, pw, ps, cw, cb, lg, lb, cwo, wo)


def _interleave_segments(t):
    *lead, seq, c = t.shape
    t = t.reshape(*lead, seq // TILE, SUBLANES, SEG, c)
    return jnp.swapaxes(t, -3, -2).reshape(*lead, seq, c)


def _deinterleave_segments(t):
    *lead, seq, c = t.shape
    t = t.reshape(*lead, seq // TILE, SEG, SUBLANES, c)
    return jnp.swapaxes(t, -3, -2).reshape(*lead, seq, c)


def kernel(x, p, ffn1_norm, ffn1_w_gate, ffn1_w_up, ffn1_w_down, mix_norm, w_in, pool_w, pool_scale, conv_dw_w, conv_dw_b, conv_ln_g, conv_ln_b, conv_w_out, w_out, ffn2_norm, ffn2_w_gate, ffn2_w_up, ffn2_w_down, ple_norm, ple_w_gate, ple_w_proj, final_norm):
    bsz, seq, d = x.shape
    depth = p.shape[0]
    n_tok = bsz * seq
    row = lambda v: v.reshape(1, -1)
    bf = lambda w: w.astype(BF16)
    h = _interleave_segments(x)
    p = _interleave_segments(p)
    for i in range(depth):
        h = _ffn_call(h.reshape(n_tok, d), row(ffn1_norm[i]), bf(ffn1_w_gate[i]),
                      bf(ffn1_w_up[i]), bf(ffn1_w_down[i]))
        h = _mix_call(h.reshape(bsz, seq, d), row(mix_norm[i]), bf(w_in[i]), bf(pool_w[i]),
                      row(pool_scale[i]), conv_dw_w[i], row(conv_dw_b[i]), row(conv_ln_g[i]),
                      row(conv_ln_b[i]), bf(conv_w_out[i]), bf(w_out[i]))
        ple_args = (p[i].reshape(n_tok, PLE_DIM), row(ple_norm[i]), bf(ple_w_gate[i]),
                    bf(ple_w_proj[i]))
        h = _ffn_call(h.reshape(n_tok, d), row(ffn2_norm[i]), bf(ffn2_w_gate[i]),
                      bf(ffn2_w_up[i]), bf(ffn2_w_down[i]), ple_args=ple_args,
                      final_norm=row(final_norm) if i == depth - 1 else None)
    return _deinterleave_segments(h.reshape(bsz, seq, d))
```

```python
import functools

import jax
import jax.numpy as jnp
from jax import lax
from jax.experimental import pallas as pl
from jax.experimental.pallas import tpu as pltpu

D_MODEL = 1024
PLE_DIM = 256
D_FF = 2816
N_POOL_GROUPS = 4
POOL_GROUP = D_MODEL // N_POOL_GROUPS
POOL_WINDOWS = (2, 4, 8, 16)
CONV_K = 31
RMS_EPS = 1e-6
LN_EPS = 1e-5

F32 = jnp.float32
BF16 = jnp.bfloat16

SUBLANES = 8
LANES = 128
TILE = 512
SEG = TILE // SUBLANES
LANE_BLOCKS = D_MODEL // LANES
FF_CHUNKS = ((0, 512), (512, 512), (1024, 512), (1536, 512), (2048, 512), (2560, 256))
POOL_HALO = 16
CONV_HALO = 32
VMEM_LIMIT = 56 * 1024 * 1024
MIX_VMEM_LIMIT = 60 * 1024 * 1024


def _dot(a, b):
    return jnp.dot(a, b, preferred_element_type=F32)


def _rms(x, g):
    ms = jnp.mean(x * x, axis=-1, keepdims=True)
    return x * lax.rsqrt(ms + RMS_EPS) * g


def _const_spec(shape):
    nd = len(shape)
    return pl.BlockSpec(shape, lambda *_: (0,) * nd, pipeline_mode=pl.Buffered(1))


def _lane_block(v, l):
    return v[..., l * LANES:(l + 1) * LANES]


def _interleaved_start(r):
    rows_per_seg = SEG // SUBLANES
    return SUBLANES * SUBLANES * (r % rows_per_seg) + r // rows_per_seg


def _to_segment_order(val, slab):
    nblk = val.shape[1] // LANES
    for l in range(nblk):
        for r in range(TILE // SUBLANES):
            rows = pl.ds(_interleaved_start(r), SUBLANES, stride=SUBLANES)
            slab[l, rows, :] = _lane_block(val[r * SUBLANES:(r + 1) * SUBLANES], l)
    return jnp.concatenate([slab[l] for l in range(nblk)], axis=-1)


def _to_natural_order(val, slab):
    nblk = val.shape[1] // LANES
    for l in range(nblk):
        slab[l] = _lane_block(val, l)
    out_rows = []
    for r in range(TILE // SUBLANES):
        rows = pl.ds(_interleaved_start(r), SUBLANES, stride=SUBLANES)
        out_rows.append(jnp.concatenate([slab[l, rows, :] for l in range(nblk)], axis=-1))
    return jnp.concatenate(out_rows, axis=0)


def _ffn_kernel(*refs, ple, final, to_segment, to_natural):
    h_ref, g_ref, wg_ref, wu_ref, wd_ref = refs[:5]
    n_in = 5 + (4 if ple else 0) + (1 if final else 0)
    o_ref = refs[n_in]
    slab = refs[n_in + 1] if len(refs) > n_in + 1 else None
    h = h_ref[...]
    xn = _rms(h, g_ref[...]).astype(BF16)
    acc = None
    for c0, cw in FF_CHUNKS:
        g = _dot(xn, wg_ref[:, c0:c0 + cw])
        u = _dot(xn, wu_ref[:, c0:c0 + cw])
        a = (g * jax.nn.sigmoid(g) * u).astype(BF16)
        d = _dot(a, wd_ref[c0:c0 + cw, :])
        acc = d if acc is None else acc + d
    out = h + 0.5 * acc
    if ple:
        p_ref, pn_ref, pwg_ref, pwp_ref = refs[5:9]
        p_tile = _to_segment_order(p_ref[...], slab)
        hn = _rms(out, pn_ref[...]).astype(BF16)
        gate = jax.nn.sigmoid(_dot(hn, pwg_ref[...]))
        pe = _dot(p_tile.astype(BF16), pwp_ref[...])
        out = out + gate * pe
    if final:
        fn_ref = refs[n_in - 1]
        out = _rms(out, fn_ref[...])
    if to_segment:
        out = _to_segment_order(out, slab)
    if to_natural:
        out = _to_natural_order(out, slab)
    o_ref[...] = out


def _ffn_call(h, norm, wg, wu, wd, ple_args=None, final_norm=None, to_segment=False,
              to_natural=False):
    n_tok = h.shape[0]
    n_tiles = n_tok // TILE
    tok_spec = pl.BlockSpec((TILE, D_MODEL), lambda i: (i, 0))
    in_specs = [tok_spec, _const_spec((1, D_MODEL)), _const_spec((D_MODEL, D_FF)),
                _const_spec((D_MODEL, D_FF)), _const_spec((D_FF, D_MODEL))]
    args = [h, norm, wg, wu, wd]
    if ple_args is not None:
        p_all, layer, pn, pwg, pwp = ple_args
        in_specs += [pl.BlockSpec((TILE, PLE_DIM), lambda i: (layer * n_tiles + i, 0)),
                     _const_spec((1, D_MODEL)), _const_spec((D_MODEL, D_MODEL)),
                     _const_spec((PLE_DIM, D_MODEL))]
        args += [p_all, pn, pwg, pwp]
    if final_norm is not None:
        in_specs.append(_const_spec((1, D_MODEL)))
        args.append(final_norm)
    needs_slab = ple_args is not None or to_segment or to_natural
    scratch = [pltpu.VMEM((LANE_BLOCKS, TILE, LANES), F32)] if needs_slab else []
    return pl.pallas_call(
        functools.partial(_ffn_kernel, ple=ple_args is not None, final=final_norm is not None,
                          to_segment=to_segment, to_natural=to_natural),
        grid=(n_tiles,),
        in_specs=in_specs,
        out_specs=tok_spec,
        out_shape=jax.ShapeDtypeStruct((n_tok, D_MODEL), F32),
        scratch_shapes=scratch,
        compiler_params=pltpu.CompilerParams(
            dimension_semantics=("arbitrary",), vmem_limit_bytes=VMEM_LIMIT),
        name="ffn_ple" if ple_args is not None else "ffn",
    )(*args)


COL_CHUNK = 256
CHUNKS_PER_COL = D_MODEL // COL_CHUNK
LOOP_CHUNKS_PER_BLOCK = 2
LOOP_CHUNKS = LOOP_CHUNKS_PER_BLOCK * LANE_BLOCKS
COL_ORDER = (1, 2, 0, 3, 4)
N_COL_CHUNKS = len(COL_ORDER) * CHUNKS_PER_COL
CONV_ACC_ROWS = 16
TAP_GROUPS = ((0, 16), (16, CONV_K))


def _with_history(cur, carry_ref, halo):
    rolled = pltpu.roll(cur[SEG - halo:], 1, axis=1)
    sub = lax.broadcasted_iota(jnp.int32, rolled.shape, 1)
    hist = jnp.where(sub == 0, carry_ref[...], rolled)
    carry_ref[...] = rolled
    return hist


def _mix_kernel(h_ref, nrm_ref, win_ref, pw_ref, ps_ref, cw_ref, cb_ref, lg_ref, lb_ref,
                cwo_ref, wo_ref, o_ref, u_buf, z_buf, c_buf, y_buf, zp_buf, pa_buf, sgc_buf,
                h_buf, zp_carry, c_carry, w_rows, *, tiles_per_seq):
    t = pl.program_id(0)
    s = lax.rem(t, tiles_per_seq)

    @pl.when(t == 0)
    def _():
        c_buf[...] = jnp.zeros(c_buf.shape, F32)
        pa_buf[...] = jnp.zeros(pa_buf.shape, F32)
        sgc_buf[...] = jnp.zeros(sgc_buf.shape, F32)
        h_buf[...] = jnp.zeros(h_buf.shape, F32)
        for k in range(CONV_K):
            wk = jnp.broadcast_to(cw_ref[k:k + 1, :], (SUBLANES, D_MODEL))
            for l in range(LANE_BLOCKS):
                w_rows[l, k] = _lane_block(wk, l)

    @pl.when(s == 0)
    def _():
        zp_carry[...] = jnp.zeros(zp_carry.shape, F32)
        c_carry[...] = jnp.zeros(c_carry.shape, F32)

    h = h_ref[...]
    u_buf[...] = _rms(h, nrm_ref[...]).astype(BF16)

    off = CONV_HALO - (CONV_K - 1)

    def lane_block_step(l):
        for q in range(LOOP_CHUNKS_PER_BLOCK):
            idx = l * LOOP_CHUNKS_PER_BLOCK + q
            z_buf[idx] = _dot(u_buf[...], win_ref[idx])
        for r0 in range(0, SEG, CONV_ACC_ROWS):
            acc = [None] * CONV_ACC_ROWS
            for k0, k1 in TAP_GROUPS:
                w = {k: w_rows[l, k] for k in range(k0, k1)}
                for rho in range(k0, CONV_ACC_ROWS + k1 - 1):
                    crow = c_buf[l, r0 + off + rho]
                    for r in range(CONV_ACC_ROWS):
                        k = rho - r
                        if k0 <= k < k1:
                            prod = crow * w[k]
                            acc[r] = prod if acc[r] is None else acc[r] + prod
            for r in range(CONV_ACC_ROWS):
                y_buf[l, r0 + r] = acc[r]

    def lane_block_pair(i, carry):
        lane_block_step(2 * i)
        lane_block_step(2 * i + 1)
        return carry

    lax.fori_loop(0, LANE_BLOCKS // 2, lane_block_pair, 0)

    def zcol(j):
        return jnp.concatenate([z_buf[j * CHUNKS_PER_COL + q] for q in range(CHUNKS_PER_COL)],
                               axis=-1)

    u = u_buf[...]
    g_conv = jnp.concatenate([_dot(u, win_ref[LOOP_CHUNKS + q]) for q in range(CHUNKS_PER_COL)],
                             axis=-1)

    y = jnp.concatenate([y_buf[l] for l in range(LANE_BLOCKS)], axis=-1).reshape(TILE, D_MODEL)
    y = y + cb_ref[...]
    mu = jnp.mean(y, axis=-1, keepdims=True)
    yc = y - mu
    var = jnp.mean(yc * yc, axis=-1, keepdims=True)
    yn = yc * lax.rsqrt(var + LN_EPS) * lg_ref[...] + lb_ref[...]
    cm = _dot((yn * jax.nn.sigmoid(yn)).astype(BF16), cwo_ref[...])

    c = (zcol(0) * jax.nn.sigmoid(zcol(1))).reshape(SEG, SUBLANES, D_MODEL)
    c_hist = _with_history(c, c_carry, CONV_HALO)
    for l in range(LANE_BLOCKS):
        c_buf[l, 0:CONV_HALO] = _lane_block(c_hist, l)
        c_buf[l, CONV_HALO:] = _lane_block(c, l)

    zp = zcol(2).reshape(SEG, SUBLANES, D_MODEL)
    zp_buf[0:POOL_HALO] = _with_history(zp, zp_carry, POOL_HALO)
    zp_buf[POOL_HALO:] = zp
    shp = (SEG, SUBLANES, POOL_GROUP)
    pos = (s * TILE + lax.broadcasted_iota(jnp.int32, shp, 1) * SEG
           + lax.broadcasted_iota(jnp.int32, shp, 0))
    a_parts = []
    for g, w in enumerate(POOL_WINDOWS):
        zg = zp_buf[:, :, g * POOL_GROUP:(g + 1) * POOL_GROUP]
        ssum, span = zg, 1
        while span < w:
            ssum = ssum[span:] + ssum[:-span]
            span *= 2
        ssum = ssum[POOL_HALO - (w - 1):]
        count = jnp.minimum(pos + 1, w).astype(F32)
        pooled = ssum / count - zg[POOL_HALO:]
        a_parts.append(_dot(pooled.reshape(TILE, POOL_GROUP).astype(BF16), pw_ref[g]))
    a = jnp.concatenate(a_parts, axis=-1) * ps_ref[...]
    pa_new = jax.nn.sigmoid(zcol(3)) * a
    sgc_new = jax.nn.sigmoid(g_conv)

    m = pa_buf[...] + sgc_buf[...] * cm
    o_ref[...] = h_buf[...] + _dot(m.astype(BF16), wo_ref[...])

    pa_buf[...] = pa_new
    sgc_buf[...] = sgc_new
    h_buf[...] = h


def _mix_call(h, seq, nrm, win, pw, ps, cw, cb, lg, lb, cwo, wo):
    n_tok = h.shape[0]
    n_tiles = n_tok // TILE
    in_specs = [pl.BlockSpec((TILE, D_MODEL), lambda t: (jnp.minimum(t, n_tiles - 1), 0)),
                _const_spec((1, D_MODEL)), _const_spec((N_COL_CHUNKS, D_MODEL, COL_CHUNK)),
                _const_spec((N_POOL_GROUPS, POOL_GROUP, POOL_GROUP)), _const_spec((1, D_MODEL)),
                _const_spec((CONV_K, D_MODEL)), _const_spec((1, D_MODEL)),
                _const_spec((1, D_MODEL)), _const_spec((1, D_MODEL)),
                _const_spec((D_MODEL, D_MODEL)), _const_spec((D_MODEL, D_MODEL))]
    vreg_rows = lambda n: pltpu.VMEM((n, SUBLANES, D_MODEL), F32)
    tile_f32 = pltpu.VMEM((TILE, D_MODEL), F32)
    scratch = [pltpu.VMEM((TILE, D_MODEL), BF16),
               pltpu.VMEM((LOOP_CHUNKS, TILE, COL_CHUNK), F32),
               pltpu.VMEM((LANE_BLOCKS, CONV_HALO + SEG, SUBLANES, LANES), F32),
               pltpu.VMEM((LANE_BLOCKS, SEG, SUBLANES, LANES), F32),
               vreg_rows(POOL_HALO + SEG),
               tile_f32, tile_f32, tile_f32,
               vreg_rows(POOL_HALO), vreg_rows(CONV_HALO),
               pltpu.VMEM((LANE_BLOCKS, CONV_K, SUBLANES, LANES), F32)]
    return pl.pallas_call(
        functools.partial(_mix_kernel, tiles_per_seq=seq // TILE),
        grid=(n_tiles + 1,),
        in_specs=in_specs,
        out_specs=pl.BlockSpec((TILE, D_MODEL), lambda t: (jnp.maximum(t - 1, 0), 0)),
        out_shape=jax.ShapeDtypeStruct(h.shape, F32),
        scratch_shapes=scratch,
        compiler_params=pltpu.CompilerParams(
            dimension_semantics=("arbitrary",), vmem_limit_bytes=MIX_VMEM_LIMIT),
        name="mixer",
    )(h, nrm, win, pw, ps, cw, cb, lg, lb, cwo, wo)


def _chunk_w_in(w_in):
    cols = jnp.concatenate([w_in[:, c * D_MODEL:(c + 1) * D_MODEL] for c in COL_ORDER], axis=1)
    return cols.reshape(D_MODEL, N_COL_CHUNKS, COL_CHUNK).transpose(1, 0, 2).astype(BF16)


def kernel(x, p, ffn1_norm, ffn1_w_gate, ffn1_w_up, ffn1_w_down, mix_norm, w_in, pool_w, pool_scale, conv_dw_w, conv_dw_b, conv_ln_g, conv_ln_b, conv_w_out, w_out, ffn2_norm, ffn2_w_gate, ffn2_w_up, ffn2_w_down, ple_norm, ple_w_gate, ple_w_proj, final_norm):
    bsz, seq, d = x.shape
    depth = p.shape[0]
    n_tok = bsz * seq
    row = lambda v: v.reshape(1, -1)
    bf = lambda w: w.astype(BF16)
    h = x.reshape(n_tok, d)
    p_all = p.reshape(depth * n_tok, PLE_DIM)
    for i in range(depth):
        last = i == depth - 1
        h = _ffn_call(h, row(ffn1_norm[i]), bf(ffn1_w_gate[i]), bf(ffn1_w_up[i]),
                      bf(ffn1_w_down[i]), to_segment=(i == 0))
        h = _mix_call(h, seq, row(mix_norm[i]), _chunk_w_in(w_in[i]), bf(pool_w[i]),
                      row(pool_scale[i]), conv_dw_w[i], row(conv_dw_b[i]), row(conv_ln_g[i]),
                      row(conv_ln_b[i]), bf(conv_w_out[i]), bf(w_out[i]))
        ple_args = (p_all, i, row(ple_norm[i]), bf(ple_w_gate[i]), bf(ple_w_proj[i]))
        h = _ffn_call(h, row(ffn2_norm[i]), bf(ffn2_w_gate[i]), bf(ffn2_w_up[i]),
                      bf(ffn2_w_down[i]), ple_args=ple_args,
                      final_norm=row(final_norm) if last else None, to_natural=last)
    return h.reshape(bsz, seq, d)
```

```python
import functools

import jax
import jax.numpy as jnp
from jax import lax
from jax.experimental import pallas as pl
from jax.experimental.pallas import tpu as pltpu

D_MODEL = 1024
PLE_DIM = 256
D_FF = 2816
N_POOL_GROUPS = 4
POOL_GROUP = D_MODEL // N_POOL_GROUPS
POOL_WINDOWS = (2, 4, 8, 16)
CONV_K = 31
RMS_EPS = 1e-6
LN_EPS = 1e-5

F32 = jnp.float32
BF16 = jnp.bfloat16

SUBLANES = 8
LANES = 128
TILE = 512
SEG = TILE // SUBLANES
LANE_BLOCKS = D_MODEL // LANES
FF_CHUNKS = ((0, 512), (512, 512), (1024, 512), (1536, 512), (2048, 512), (2560, 256))
POOL_HALO = 16
CONV_HALO = 32
VMEM_LIMIT = 56 * 1024 * 1024


def _dot(a, b):
    return jnp.dot(a, b, preferred_element_type=F32)


def _rms(x, g):
    ms = jnp.mean(x * x, axis=-1, keepdims=True)
    return x * lax.rsqrt(ms + RMS_EPS) * g


def _const_spec(shape):
    nd = len(shape)
    return pl.BlockSpec(shape, lambda *_: (0,) * nd, pipeline_mode=pl.Buffered(1))


def _lane_block(v, l):
    return v[..., l * LANES:(l + 1) * LANES]


def _interleaved_start(r):
    rows_per_seg = SEG // SUBLANES
    return SUBLANES * SUBLANES * (r % rows_per_seg) + r // rows_per_seg


def _to_segment_order(val, slab):
    nblk = val.shape[1] // LANES
    for l in range(nblk):
        for r in range(TILE // SUBLANES):
            rows = pl.ds(_interleaved_start(r), SUBLANES, stride=SUBLANES)
            slab[l, rows, :] = _lane_block(val[r * SUBLANES:(r + 1) * SUBLANES], l)
    return jnp.concatenate([slab[l] for l in range(nblk)], axis=-1)


def _to_natural_order(val, slab):
    nblk = val.shape[1] // LANES
    for l in range(nblk):
        slab[l] = _lane_block(val, l)
    out_rows = []
    for r in range(TILE // SUBLANES):
        rows = pl.ds(_interleaved_start(r), SUBLANES, stride=SUBLANES)
        out_rows.append(jnp.concatenate([slab[l, rows, :] for l in range(nblk)], axis=-1))
    return jnp.concatenate(out_rows, axis=0)


def _ffn_kernel(*refs, ple, final, to_segment, to_natural):
    h_ref, g_ref, wg_ref, wu_ref, wd_ref = refs[:5]
    n_in = 5 + (4 if ple else 0) + (1 if final else 0)
    o_ref = refs[n_in]
    slab = refs[n_in + 1] if len(refs) > n_in + 1 else None
    h = h_ref[...]
    xn = _rms(h, g_ref[...]).astype(BF16)
    acc = None
    for c0, cw in FF_CHUNKS:
        g = _dot(xn, wg_ref[:, c0:c0 + cw])
        u = _dot(xn, wu_ref[:, c0:c0 + cw])
        a = (g * jax.nn.sigmoid(g) * u).astype(BF16)
        d = _dot(a, wd_ref[c0:c0 + cw, :])
        acc = d if acc is None else acc + d
    out = h + 0.5 * acc
    if ple:
        p_ref, pn_ref, pwg_ref, pwp_ref = refs[5:9]
        p_tile = _to_segment_order(p_ref[...], slab)
        hn = _rms(out, pn_ref[...]).astype(BF16)
        gate = jax.nn.sigmoid(_dot(hn, pwg_ref[...]))
        pe = _dot(p_tile.astype(BF16), pwp_ref[...])
        out = out + gate * pe
    if final:
        fn_ref = refs[n_in - 1]
        out = _rms(out, fn_ref[...])
    if to_segment:
        out = _to_segment_order(out, slab)
    if to_natural:
        out = _to_natural_order(out, slab)
    o_ref[...] = out


def _ffn_call(h, norm, wg, wu, wd, ple_args=None, final_norm=None, to_segment=False,
              to_natural=False):
    n_tok = h.shape[0]
    n_tiles = n_tok // TILE
    tok_spec = pl.BlockSpec((TILE, D_MODEL), lambda i: (i, 0))
    in_specs = [tok_spec, _const_spec((1, D_MODEL)), _const_spec((D_MODEL, D_FF)),
                _const_spec((D_MODEL, D_FF)), _const_spec((D_FF, D_MODEL))]
    args = [h, norm, wg, wu, wd]
    if ple_args is not None:
        p_all, layer, pn, pwg, pwp = ple_args
        in_specs += [pl.BlockSpec((TILE, PLE_DIM), lambda i: (layer * n_tiles + i, 0)),
                     _const_spec((1, D_MODEL)), _const_spec((D_MODEL, D_MODEL)),
                     _const_spec((PLE_DIM, D_MODEL))]
        args += [p_all, pn, pwg, pwp]
    if final_norm is not None:
        in_specs.append(_const_spec((1, D_MODEL)))
        args.append(final_norm)
    needs_slab = ple_args is not None or to_segment or to_natural
    scratch = [pltpu.VMEM((LANE_BLOCKS, TILE, LANES), F32)] if needs_slab else []
    return pl.pallas_call(
        functools.partial(_ffn_kernel, ple=ple_args is not None, final=final_norm is not None,
                          to_segment=to_segment, to_natural=to_natural),
        grid=(n_tiles,),
        in_specs=in_specs,
        out_specs=tok_spec,
        out_shape=jax.ShapeDtypeStruct((n_tok, D_MODEL), F32),
        scratch_shapes=scratch,
        compiler_params=pltpu.CompilerParams(
            dimension_semantics=("arbitrary",), vmem_limit_bytes=VMEM_LIMIT),
        name="ffn_ple" if ple_args is not None else "ffn",
    )(*args)


CONV_ROWS = 8
GATE_CHUNKS = SEG // CONV_ROWS
GATE_CHUNK = 2 * D_MODEL // GATE_CHUNKS


def _with_history(cur, carry_ref, halo):
    rolled = pltpu.roll(cur[SEG - halo:], 1, axis=1)
    sub = lax.broadcasted_iota(jnp.int32, rolled.shape, 1)
    hist = jnp.where(sub == 0, carry_ref[...], rolled)
    carry_ref[...] = rolled
    return hist


def _mix_kernel(h_ref, nrm_ref, win_ref, wgate_ref, pw_ref, ps_ref, cw_ref, cb_ref, lg_ref,
                lb_ref, cwo_ref, wo_ref, o_ref, u_buf, g_buf, zp_buf, c_buf, y_buf, zp_carry,
                c_carry, w_rows):
    s = pl.program_id(1)

    @pl.when(s == 0)
    def _():
        zp_carry[...] = jnp.zeros(zp_carry.shape, F32)
        c_carry[...] = jnp.zeros(c_carry.shape, F32)
        for k in range(CONV_K):
            w_rows[k] = jnp.broadcast_to(cw_ref[k:k + 1, :], (SUBLANES, D_MODEL))

    h = h_ref[...]
    u = _rms(h, nrm_ref[...]).astype(BF16)
    u_buf[...] = u

    def zcol(i):
        return _dot(u, win_ref[:, i * D_MODEL:(i + 1) * D_MODEL])

    zp = zcol(0).reshape(SEG, SUBLANES, D_MODEL)
    zp_buf[0:POOL_HALO] = _with_history(zp, zp_carry, POOL_HALO)
    zp_buf[POOL_HALO:] = zp
    c = (zcol(1) * jax.nn.sigmoid(zcol(2))).reshape(SEG, SUBLANES, D_MODEL)
    c_buf[0:CONV_HALO] = _with_history(c, c_carry, CONV_HALO)
    c_buf[CONV_HALO:] = c

    shp = (SEG, SUBLANES, POOL_GROUP)
    pos = (s * TILE + lax.broadcasted_iota(jnp.int32, shp, 1) * SEG
           + lax.broadcasted_iota(jnp.int32, shp, 0))
    a_parts = []
    for g, w in enumerate(POOL_WINDOWS):
        zg = zp_buf[:, :, g * POOL_GROUP:(g + 1) * POOL_GROUP]
        ssum, span = zg, 1
        while span < w:
            ssum = ssum[span:] + ssum[:-span]
            span *= 2
        ssum = ssum[POOL_HALO - (w - 1):]
        count = jnp.minimum(pos + 1, w).astype(F32)
        pooled = ssum / count - zg[POOL_HALO:]
        a_parts.append(_dot(pooled.reshape(TILE, POOL_GROUP).astype(BF16), pw_ref[g]))
    a = jnp.concatenate(a_parts, axis=-1) * ps_ref[...]

    off = CONV_HALO - (CONV_K - 1)

    def conv_rows(i, carry):
        r0 = i * CONV_ROWS
        for l in range(LANE_BLOCKS):
            lanes = slice(l * LANES, (l + 1) * LANES)
            acc = None
            for k in range(CONV_K):
                t = c_buf[pl.ds(r0 + off + k, CONV_ROWS), :, lanes] * w_rows[k, :, lanes][None]
                acc = t if acc is None else acc + t
            y_buf[pl.ds(r0, CONV_ROWS), :, lanes] = acc
        g_buf[i] = _dot(u_buf[...], wgate_ref[i])
        return carry

    lax.fori_loop(0, SEG // CONV_ROWS, conv_rows, 0)

    y = y_buf[...].reshape(TILE, D_MODEL) + cb_ref[...]
    mu = jnp.mean(y, axis=-1, keepdims=True)
    yc = y - mu
    var = jnp.mean(yc * yc, axis=-1, keepdims=True)
    yn = yc * lax.rsqrt(var + LN_EPS) * lg_ref[...] + lb_ref[...]
    cm = _dot((yn * jax.nn.sigmoid(yn)).astype(BF16), cwo_ref[...])

    half = GATE_CHUNKS // 2
    g_pool = jnp.concatenate([g_buf[q] for q in range(half)], axis=-1)
    g_conv = jnp.concatenate([g_buf[half + q] for q in range(half)], axis=-1)
    m = jax.nn.sigmoid(g_pool) * a + jax.nn.sigmoid(g_conv) * cm
    o_ref[...] = h + _dot(m.astype(BF16), wo_ref[...])


def _mix_call(h, seq, nrm, win, wgate, pw, ps, cw, cb, lg, lb, cwo, wo):
    n_tok = h.shape[0]
    tiles_per_seq = seq // TILE
    tok_spec = pl.BlockSpec((TILE, D_MODEL), lambda b, s: (b * tiles_per_seq + s, 0))
    in_specs = [tok_spec, _const_spec((1, D_MODEL)), _const_spec((D_MODEL, 3 * D_MODEL)),
                _const_spec((GATE_CHUNKS, D_MODEL, GATE_CHUNK)),
                _const_spec((N_POOL_GROUPS, POOL_GROUP, POOL_GROUP)), _const_spec((1, D_MODEL)),
                _const_spec((CONV_K, D_MODEL)), _const_spec((1, D_MODEL)),
                _const_spec((1, D_MODEL)), _const_spec((1, D_MODEL)),
                _const_spec((D_MODEL, D_MODEL)), _const_spec((D_MODEL, D_MODEL))]
    vreg_rows = lambda n: pltpu.VMEM((n, SUBLANES, D_MODEL), F32)
    return pl.pallas_call(
        _mix_kernel,
        grid=(n_tok // seq, tiles_per_seq),
        in_specs=in_specs,
        out_specs=tok_spec,
        out_shape=jax.ShapeDtypeStruct(h.shape, F32),
        scratch_shapes=[pltpu.VMEM((TILE, D_MODEL), BF16),
                        pltpu.VMEM((GATE_CHUNKS, TILE, GATE_CHUNK), F32),
                        vreg_rows(POOL_HALO + SEG), vreg_rows(CONV_HALO + SEG), vreg_rows(SEG),
                        vreg_rows(POOL_HALO), vreg_rows(CONV_HALO), vreg_rows(CONV_K)],
        compiler_params=pltpu.CompilerParams(
            dimension_semantics=("arbitrary", "arbitrary"), vmem_limit_bytes=VMEM_LIMIT),
        name="mixer",
    )(h, nrm, win, wgate, pw, ps, cw, cb, lg, lb, cwo, wo)


def _split_w_in(w_in):
    main = w_in[:, :3 * D_MODEL].astype(BF16)
    gates = w_in[:, 3 * D_MODEL:].reshape(D_MODEL, GATE_CHUNKS, GATE_CHUNK)
    return main, gates.transpose(1, 0, 2).astype(BF16)


def kernel(x, p, ffn1_norm, ffn1_w_gate, ffn1_w_up, ffn1_w_down, mix_norm, w_in, pool_w, pool_scale, conv_dw_w, conv_dw_b, conv_ln_g, conv_ln_b, conv_w_out, w_out, ffn2_norm, ffn2_w_gate, ffn2_w_up, ffn2_w_down, ple_norm, ple_w_gate, ple_w_proj, final_norm):
    bsz, seq, d = x.shape
    depth = p.shape[0]
    n_tok = bsz * seq
    row = lambda v: v.reshape(1, -1)
    bf = lambda w: w.astype(BF16)
    h = x.reshape(n_tok, d)
    p_all = p.reshape(depth * n_tok, PLE_DIM)
    for i in range(depth):
        last = i == depth - 1
        h = _ffn_call(h, row(ffn1_norm[i]), bf(ffn1_w_gate[i]), bf(ffn1_w_up[i]),
                      bf(ffn1_w_down[i]), to_segment=(i == 0))
        h = _mix_call(h, seq, row(mix_norm[i]), *_split_w_in(w_in[i]), bf(pool_w[i]),
                      row(pool_scale[i]), conv_dw_w[i], row(conv_dw_b[i]), row(conv_ln_g[i]),
                      row(conv_ln_b[i]), bf(conv_w_out[i]), bf(w_out[i]))
        ple_args = (p_all, i, row(ple_norm[i]), bf(ple_w_gate[i]), bf(ple_w_proj[i]))
        h = _ffn_call(h, row(ffn2_norm[i]), bf(ffn2_w_gate[i]), bf(ffn2_w_up[i]),
                      bf(ffn2_w_down[i]), ple_args=ple_args,
                      final_norm=row(final_norm) if last else None, to_natural=last)
    return h.reshape(bsz, seq, d)
```

```python
import functools

import jax
import jax.numpy as jnp
from jax import lax
from jax.experimental import pallas as pl
from jax.experimental.pallas import tpu as pltpu

D_MODEL = 1024
PLE_DIM = 256
D_FF = 2816
N_POOL_GROUPS = 4
POOL_GROUP = D_MODEL // N_POOL_GROUPS
POOL_WINDOWS = (2, 4, 8, 16)
CONV_K = 31
RMS_EPS = 1e-6
LN_EPS = 1e-5

F32 = jnp.float32
BF16 = jnp.bfloat16

SUBLANES = 8
LANES = 128
TILE = 512
SEG = TILE // SUBLANES
LANE_BLOCKS = D_MODEL // LANES
FF_CHUNKS = ((0, 512), (512, 512), (1024, 512), (1536, 512), (2048, 512), (2560, 256))
POOL_HALO = 16
CONV_HALO = 32
VMEM_LIMIT = 56 * 1024 * 1024


def _dot(a, b):
    return jnp.dot(a, b, preferred_element_type=F32)


def _rms(x, g):
    ms = jnp.mean(x * x, axis=-1, keepdims=True)
    return x * lax.rsqrt(ms + RMS_EPS) * g


def _const_spec(shape):
    nd = len(shape)
    return pl.BlockSpec(shape, lambda *_: (0,) * nd, pipeline_mode=pl.Buffered(1))


def _lane_block(v, l):
    return v[..., l * LANES:(l + 1) * LANES]


def _interleaved_start(r):
    rows_per_seg = SEG // SUBLANES
    return SUBLANES * SUBLANES * (r % rows_per_seg) + r // rows_per_seg


def _to_segment_order(val, slab):
    nblk = val.shape[1] // LANES
    for l in range(nblk):
        for r in range(TILE // SUBLANES):
            rows = pl.ds(_interleaved_start(r), SUBLANES, stride=SUBLANES)
            slab[l, rows, :] = _lane_block(val[r * SUBLANES:(r + 1) * SUBLANES], l)
    return jnp.concatenate([slab[l] for l in range(nblk)], axis=-1)


def _to_natural_order(val, slab):
    nblk = val.shape[1] // LANES
    for l in range(nblk):
        slab[l] = _lane_block(val, l)
    out_rows = []
    for r in range(TILE // SUBLANES):
        rows = pl.ds(_interleaved_start(r), SUBLANES, stride=SUBLANES)
        out_rows.append(jnp.concatenate([slab[l, rows, :] for l in range(nblk)], axis=-1))
    return jnp.concatenate(out_rows, axis=0)


def _ffn_kernel(*refs, ple, final, to_segment, to_natural):
    h_ref, g_ref, wg_ref, wu_ref, wd_ref = refs[:5]
    n_in = 5 + (4 if ple else 0) + (1 if final else 0)
    o_ref = refs[n_in]
    slab = refs[n_in + 1] if len(refs) > n_in + 1 else None
    h = h_ref[...]
    xn = _rms(h, g_ref[...]).astype(BF16)
    acc = None
    for c0, cw in FF_CHUNKS:
        g = _dot(xn, wg_ref[:, c0:c0 + cw])
        u = _dot(xn, wu_ref[:, c0:c0 + cw])
        a = (g * jax.nn.sigmoid(g) * u).astype(BF16)
        d = _dot(a, wd_ref[c0:c0 + cw, :])
        acc = d if acc is None else acc + d
    out = h + 0.5 * acc
    if ple:
        p_ref, pn_ref, pwg_ref, pwp_ref = refs[5:9]
        p_tile = _to_segment_order(p_ref[...], slab)
        hn = _rms(out, pn_ref[...]).astype(BF16)
        gate = jax.nn.sigmoid(_dot(hn, pwg_ref[...]))
        pe = _dot(p_tile.astype(BF16), pwp_ref[...])
        out = out + gate * pe
    if final:
        fn_ref = refs[n_in - 1]
        out = _rms(out, fn_ref[...])
    if to_segment:
        out = _to_segment_order(out, slab)
    if to_natural:
        out = _to_natural_order(out, slab)
    o_ref[...] = out


def _ffn_call(h, norm, wg, wu, wd, ple_args=None, final_norm=None, to_segment=False,
              to_natural=False):
    n_tok = h.shape[0]
    n_tiles = n_tok // TILE
    tok_spec = pl.BlockSpec((TILE, D_MODEL), lambda i: (i, 0))
    in_specs = [tok_spec, _const_spec((1, D_MODEL)), _const_spec((D_MODEL, D_FF)),
                _const_spec((D_MODEL, D_FF)), _const_spec((D_FF, D_MODEL))]
    args = [h, norm, wg, wu, wd]
    if ple_args is not None:
        p_all, layer, pn, pwg, pwp = ple_args
        in_specs += [pl.BlockSpec((TILE, PLE_DIM), lambda i: (layer * n_tiles + i, 0)),
                     _const_spec((1, D_MODEL)), _const_spec((D_MODEL, D_MODEL)),
                     _const_spec((PLE_DIM, D_MODEL))]
        args += [p_all, pn, pwg, pwp]
    if final_norm is not None:
        in_specs.append(_const_spec((1, D_MODEL)))
        args.append(final_norm)
    needs_slab = ple_args is not None or to_segment or to_natural
    scratch = [pltpu.VMEM((LANE_BLOCKS, TILE, LANES), F32)] if needs_slab else []
    return pl.pallas_call(
        functools.partial(_ffn_kernel, ple=ple_args is not None, final=final_norm is not None,
                          to_segment=to_segment, to_natural=to_natural),
        grid=(n_tiles,),
        in_specs=in_specs,
        out_specs=tok_spec,
        out_shape=jax.ShapeDtypeStruct((n_tok, D_MODEL), F32),
        scratch_shapes=scratch,
        compiler_params=pltpu.CompilerParams(
            dimension_semantics=("arbitrary",), vmem_limit_bytes=VMEM_LIMIT),
        name="ffn_ple" if ple_args is not None else "ffn",
    )(*args)


PAIR_HALO = CONV_HALO // 2
PAIRS = SEG // 2
EVEN_TAPS = (CONV_K + 1) // 2
ODD_TAPS = CONV_K // 2
CONV_PAIRS = 16


def _with_history(cur, carry_ref, halo):
    rolled = pltpu.roll(cur[SEG - halo:], 1, axis=1)
    sub = lax.broadcasted_iota(jnp.int32, rolled.shape, 1)
    hist = jnp.where(sub == 0, carry_ref[...], rolled)
    carry_ref[...] = rolled
    return hist


def _mix_kernel(h_ref, nrm_ref, win_ref, pw_ref, ps_ref, cw_ref, cb_ref, lg_ref, lb_ref,
                cwo_ref, wo_ref, o_ref, zp_buf, x0_buf, x1_buf, xs_buf, y_buf, zp_carry, c_carry,
                g_even, g_odd, g_sum):
    s = pl.program_id(1)

    @pl.when(s == 0)
    def _():
        zp_carry[...] = jnp.zeros(zp_carry.shape, F32)
        c_carry[...] = jnp.zeros(c_carry.shape, F32)
        for l in range(LANE_BLOCKS):
            lanes = slice(l * LANES, (l + 1) * LANES)
            for e in range(EVEN_TAPS):
                k = CONV_K - 1 - 2 * e
                ge = jnp.broadcast_to(cw_ref[k:k + 1, lanes], (SUBLANES, LANES))
                g_even[l, e] = ge
                if e < ODD_TAPS:
                    go = jnp.broadcast_to(cw_ref[k - 1:k, lanes], (SUBLANES, LANES))
                    g_odd[l, e] = go
                    g_sum[l, e] = ge + go
                else:
                    g_sum[l, e] = ge

    h = h_ref[...]
    u = _rms(h, nrm_ref[...]).astype(BF16)

    def zcol(i):
        return _dot(u, win_ref[:, i * D_MODEL:(i + 1) * D_MODEL])

    zp = zcol(0).reshape(SEG, SUBLANES, D_MODEL)
    zp_buf[0:POOL_HALO] = _with_history(zp, zp_carry, POOL_HALO)
    zp_buf[POOL_HALO:] = zp
    c = (zcol(1) * jax.nn.sigmoid(zcol(2))).reshape(SEG, SUBLANES, D_MODEL)
    c_hist = _with_history(c, c_carry, CONV_HALO)
    c_pairs = jnp.concatenate([c_hist, c], axis=0).reshape(PAIR_HALO + PAIRS, 2, SUBLANES, D_MODEL)
    for l in range(LANE_BLOCKS):
        even, odd = _lane_block(c_pairs[:, 0], l), _lane_block(c_pairs[:, 1], l)
        x0_buf[l] = even
        x1_buf[l] = odd
        xs_buf[l] = even + odd

    shp = (SEG, SUBLANES, POOL_GROUP)
    pos = (s * TILE + lax.broadcasted_iota(jnp.int32, shp, 1) * SEG
           + lax.broadcasted_iota(jnp.int32, shp, 0))
    a_parts = []
    for g, w in enumerate(POOL_WINDOWS):
        zg = zp_buf[:, :, g * POOL_GROUP:(g + 1) * POOL_GROUP]
        ssum, span = zg, 1
        while span < w:
            ssum = ssum[span:] + ssum[:-span]
            span *= 2
        ssum = ssum[POOL_HALO - (w - 1):]
        count = jnp.minimum(pos + 1, w).astype(F32)
        pooled = ssum / count - zg[POOL_HALO:]
        a_parts.append(_dot(pooled.reshape(TILE, POOL_GROUP).astype(BF16), pw_ref[g]))
    a = jnp.concatenate(a_parts, axis=-1) * ps_ref[...]

    chunks = PAIRS // CONV_PAIRS

    def conv_pairs(i, carry):
        l = i // chunks
        m0 = (i % chunks) * CONV_PAIRS
        acc_a = acc_b = acc_c = None
        for e in range(EVEN_TAPS):
            t = x0_buf[l, pl.ds(m0 + PAIR_HALO - e, CONV_PAIRS)] * g_even[l, e][None]
            acc_a = t if acc_a is None else acc_a + t
        for e in range(ODD_TAPS):
            t = x1_buf[l, pl.ds(m0 + PAIR_HALO - 1 - e, CONV_PAIRS + 1)] * g_odd[l, e][None]
            acc_b = t if acc_b is None else acc_b + t
        y_buf[l, pl.ds(m0, CONV_PAIRS), 0] = acc_a + acc_b[:CONV_PAIRS]
        acc_ab = acc_a + acc_b[1:]
        for e in range(EVEN_TAPS):
            t = xs_buf[l, pl.ds(m0 + PAIR_HALO - e, CONV_PAIRS)] * g_sum[l, e][None]
            acc_c = t if acc_c is None else acc_c + t
        y_buf[l, pl.ds(m0, CONV_PAIRS), 1] = acc_c - acc_ab
        return carry

    lax.fori_loop(0, LANE_BLOCKS * chunks, conv_pairs, 0)
    y = jnp.concatenate([y_buf[l] for l in range(LANE_BLOCKS)], axis=-1).reshape(TILE, D_MODEL)
    y = y + cb_ref[...]
    mu = jnp.mean(y, axis=-1, keepdims=True)
    yc = y - mu
    var = jnp.mean(yc * yc, axis=-1, keepdims=True)
    yn = yc * lax.rsqrt(var + LN_EPS) * lg_ref[...] + lb_ref[...]
    cm = _dot((yn * jax.nn.sigmoid(yn)).astype(BF16), cwo_ref[...])

    m = jax.nn.sigmoid(zcol(3)) * a + jax.nn.sigmoid(zcol(4)) * cm
    o_ref[...] = h + _dot(m.astype(BF16), wo_ref[...])


def _mix_call(h, seq, nrm, win, pw, ps, cw, cb, lg, lb, cwo, wo):
    n_tok = h.shape[0]
    tiles_per_seq = seq // TILE
    tok_spec = pl.BlockSpec((TILE, D_MODEL), lambda b, s: (b * tiles_per_seq + s, 0))
    in_specs = [tok_spec, _const_spec((1, D_MODEL)), _const_spec((D_MODEL, win.shape[1])),
                _const_spec((N_POOL_GROUPS, POOL_GROUP, POOL_GROUP)), _const_spec((1, D_MODEL)),
                _const_spec((CONV_K, D_MODEL)), _const_spec((1, D_MODEL)),
                _const_spec((1, D_MODEL)), _const_spec((1, D_MODEL)),
                _const_spec((D_MODEL, D_MODEL)), _const_spec((D_MODEL, D_MODEL))]
    vreg_rows = lambda n: pltpu.VMEM((n, SUBLANES, D_MODEL), F32)
    slabs = lambda n: pltpu.VMEM((LANE_BLOCKS, n, SUBLANES, LANES), F32)
    return pl.pallas_call(
        _mix_kernel,
        grid=(n_tok // seq, tiles_per_seq),
        in_specs=in_specs,
        out_specs=tok_spec,
        out_shape=jax.ShapeDtypeStruct(h.shape, F32),
        scratch_shapes=[vreg_rows(POOL_HALO + SEG),
                        slabs(PAIR_HALO + PAIRS), slabs(PAIR_HALO + PAIRS),
                        slabs(PAIR_HALO + PAIRS),
                        pltpu.VMEM((LANE_BLOCKS, PAIRS, 2, SUBLANES, LANES), F32),
                        vreg_rows(POOL_HALO), vreg_rows(CONV_HALO),
                        slabs(EVEN_TAPS), slabs(ODD_TAPS), slabs(EVEN_TAPS)],
        compiler_params=pltpu.CompilerParams(
            dimension_semantics=("arbitrary", "arbitrary"), vmem_limit_bytes=VMEM_LIMIT),
        name="mixer",
    )(h, nrm, win, pw, ps, cw, cb, lg, lb, cwo, wo)


def kernel(x, p, ffn1_norm, ffn1_w_gate, ffn1_w_up, ffn1_w_down, mix_norm, w_in, pool_w, pool_scale, conv_dw_w, conv_dw_b, conv_ln_g, conv_ln_b, conv_w_out, w_out, ffn2_norm, ffn2_w_gate, ffn2_w_up, ffn2_w_down, ple_norm, ple_w_gate, ple_w_proj, final_norm):
    bsz, seq, d = x.shape
    depth = p.shape[0]
    n_tok = bsz * seq
    row = lambda v: v.reshape(1, -1)
    bf = lambda w: w.astype(BF16)
    h = x.reshape(n_tok, d)
    p_all = p.reshape(depth * n_tok, PLE_DIM)
    for i in range(depth):
        last = i == depth - 1
        h = _ffn_call(h, row(ffn1_norm[i]), bf(ffn1_w_gate[i]), bf(ffn1_w_up[i]),
                      bf(ffn1_w_down[i]), to_segment=(i == 0))
        h = _mix_call(h, seq, row(mix_norm[i]), bf(w_in[i]), bf(pool_w[i]),
                      row(pool_scale[i]), conv_dw_w[i], row(conv_dw_b[i]), row(conv_ln_g[i]),
                      row(conv_ln_b[i]), bf(conv_w_out[i]), bf(w_out[i]))
        ple_args = (p_all, i, row(ple_norm[i]), bf(ple_w_gate[i]), bf(ple_w_proj[i]))
        h = _ffn_call(h, row(ffn2_norm[i]), bf(ffn2_w_gate[i]), bf(ffn2_w_up[i]),
                      bf(ffn2_w_down[i]), ple_args=ple_args,
                      final_norm=row(final_norm) if last else None, to_natural=last)
    return h.reshape(bsz, seq, d)
```

```python
import functools

import jax
import jax.numpy as jnp
from jax import lax
from jax.experimental import pallas as pl
from jax.experimental.pallas import tpu as pltpu

D_MODEL = 1024
PLE_DIM = 256
D_FF = 2816
N_POOL_GROUPS = 4
POOL_GROUP = D_MODEL // N_POOL_GROUPS
POOL_WINDOWS = (2, 4, 8, 16)
CONV_K = 31
RMS_EPS = 1e-6
LN_EPS = 1e-5

F32 = jnp.float32
BF16 = jnp.bfloat16

SUBLANES = 8
LANES = 128
TILE = 512
SEG = TILE // SUBLANES
LANE_BLOCKS = D_MODEL // LANES
FF_CHUNK = 256
FF_CHUNKS = tuple((c0, FF_CHUNK) for c0 in range(0, D_FF, FF_CHUNK))
POOL_HALO = 16
CONV_HALO = 32
VMEM_LIMIT = 56 * 1024 * 1024


def _dot(a, b):
    return jnp.dot(a, b, preferred_element_type=F32)


def _rms(x, g):
    ms = jnp.mean(x * x, axis=-1, keepdims=True)
    return x * lax.rsqrt(ms + RMS_EPS) * g


def _const_spec(shape):
    nd = len(shape)
    return pl.BlockSpec(shape, lambda *_: (0,) * nd, pipeline_mode=pl.Buffered(1))


def _lane_block(v, l):
    return v[..., l * LANES:(l + 1) * LANES]


def _interleaved_start(r):
    rows_per_seg = SEG // SUBLANES
    return SUBLANES * SUBLANES * (r % rows_per_seg) + r // rows_per_seg


def _to_segment_order(val, slab):
    nblk = val.shape[1] // LANES
    for l in range(nblk):
        for r in range(TILE // SUBLANES):
            rows = pl.ds(_interleaved_start(r), SUBLANES, stride=SUBLANES)
            slab[l, rows, :] = _lane_block(val[r * SUBLANES:(r + 1) * SUBLANES], l)
    return jnp.concatenate([slab[l] for l in range(nblk)], axis=-1)


def _to_natural_order(val, slab):
    nblk = val.shape[1] // LANES
    for l in range(nblk):
        slab[l] = _lane_block(val, l)
    out_rows = []
    for r in range(TILE // SUBLANES):
        rows = pl.ds(_interleaved_start(r), SUBLANES, stride=SUBLANES)
        out_rows.append(jnp.concatenate([slab[l, rows, :] for l in range(nblk)], axis=-1))
    return jnp.concatenate(out_rows, axis=0)


def _ffn_kernel(*refs, ple, final, to_segment, to_natural):
    h_ref, g_ref, wg_ref, wu_ref, wd_ref = refs[:5]
    n_in = 5 + (4 if ple else 0) + (1 if final else 0)
    o_ref = refs[n_in]
    slab = refs[n_in + 1] if len(refs) > n_in + 1 else None
    h = h_ref[...]
    xn = _rms(h, g_ref[...]).astype(BF16)
    acc = None
    for c0, cw in FF_CHUNKS:
        g = _dot(xn, wg_ref[:, c0:c0 + cw])
        u = _dot(xn, wu_ref[:, c0:c0 + cw])
        a = (g * jax.nn.sigmoid(g) * u).astype(BF16)
        d = _dot(a, wd_ref[c0:c0 + cw, :])
        acc = d if acc is None else acc + d
    out = h + 0.5 * acc
    if ple:
        p_ref, pn_ref, pwg_ref, pwp_ref = refs[5:9]
        p_tile = _to_segment_order(p_ref[...], slab)
        hn = _rms(out, pn_ref[...]).astype(BF16)
        gate = jax.nn.sigmoid(_dot(hn, pwg_ref[...]))
        pe = _dot(p_tile.astype(BF16), pwp_ref[...])
        out = out + gate * pe
    if final:
        fn_ref = refs[n_in - 1]
        out = _rms(out, fn_ref[...])
    if to_segment:
        out = _to_segment_order(out, slab)
    if to_natural:
        out = _to_natural_order(out, slab)
    o_ref[...] = out


def _ffn_call(h, norm, wg, wu, wd, ple_args=None, final_norm=None, to_segment=False,
              to_natural=False):
    n_tok = h.shape[0]
    n_tiles = n_tok // TILE
    tok_spec = pl.BlockSpec((TILE, D_MODEL), lambda i: (i, 0))
    in_specs = [tok_spec, _const_spec((1, D_MODEL)), _const_spec((D_MODEL, D_FF)),
                _const_spec((D_MODEL, D_FF)), _const_spec((D_FF, D_MODEL))]
    args = [h, norm, wg, wu, wd]
    if ple_args is not None:
        p_all, layer, pn, pwg, pwp = ple_args
        in_specs += [pl.BlockSpec((TILE, PLE_DIM), lambda i: (layer * n_tiles + i, 0)),
                     _const_spec((1, D_MODEL)), _const_spec((D_MODEL, D_MODEL)),
                     _const_spec((PLE_DIM, D_MODEL))]
        args += [p_all, pn, pwg, pwp]
    if final_norm is not None:
        in_specs.append(_const_spec((1, D_MODEL)))
        args.append(final_norm)
    needs_slab = ple_args is not None or to_segment or to_natural
    scratch = [pltpu.VMEM((LANE_BLOCKS, TILE, LANES), F32)] if needs_slab else []
    return pl.pallas_call(
        functools.partial(_ffn_kernel, ple=ple_args is not None, final=final_norm is not None,
                          to_segment=to_segment, to_natural=to_natural),
        grid=(n_tiles,),
        in_specs=in_specs,
        out_specs=tok_spec,
        out_shape=jax.ShapeDtypeStruct((n_tok, D_MODEL), F32),
        scratch_shapes=scratch,
        compiler_params=pltpu.CompilerParams(
            dimension_semantics=("arbitrary",), vmem_limit_bytes=VMEM_LIMIT),
        name="ffn_ple" if ple_args is not None else "ffn",
    )(*args)


PAIR_HALO = CONV_HALO // 2
PAIRS = SEG // 2
EVEN_TAPS = (CONV_K + 1) // 2
ODD_TAPS = CONV_K // 2
CONV_PAIRS = 16


def _with_history(cur, carry_ref, halo):
    rolled = pltpu.roll(cur[SEG - halo:], 1, axis=1)
    sub = lax.broadcasted_iota(jnp.int32, rolled.shape, 1)
    hist = jnp.where(sub == 0, carry_ref[...], rolled)
    carry_ref[...] = rolled
    return hist


def _mix_kernel(h_ref, nrm_ref, win_ref, pw_ref, ps_ref, cw_ref, cb_ref, lg_ref, lb_ref,
                cwo_ref, wo_ref, o_ref, zp_buf, x0_buf, x1_buf, xs_buf, y_buf, zp_carry, c_carry,
                g_even, g_odd, g_sum):
    s = pl.program_id(1)

    @pl.when(s == 0)
    def _():
        zp_carry[...] = jnp.zeros(zp_carry.shape, F32)
        c_carry[...] = jnp.zeros(c_carry.shape, F32)
        for l in range(LANE_BLOCKS):
            lanes = slice(l * LANES, (l + 1) * LANES)
            for e in range(EVEN_TAPS):
                k = CONV_K - 1 - 2 * e
                ge = jnp.broadcast_to(cw_ref[k:k + 1, lanes], (SUBLANES, LANES))
                g_even[l, e] = ge
                if e < ODD_TAPS:
                    go = jnp.broadcast_to(cw_ref[k - 1:k, lanes], (SUBLANES, LANES))
                    g_odd[l, e] = go
                    g_sum[l, e] = ge + go
                else:
                    g_sum[l, e] = ge

    h = h_ref[...]
    u = _rms(h, nrm_ref[...]).astype(BF16)

    def zcol(i):
        return _dot(u, win_ref[:, i * D_MODEL:(i + 1) * D_MODEL])

    zp = zcol(0).reshape(SEG, SUBLANES, D_MODEL)
    zp_buf[0:POOL_HALO] = _with_history(zp, zp_carry, POOL_HALO)
    zp_buf[POOL_HALO:] = zp
    c = (zcol(1) * jax.nn.sigmoid(zcol(2))).reshape(SEG, SUBLANES, D_MODEL)
    c_hist = _with_history(c, c_carry, CONV_HALO)
    c_pairs = jnp.concatenate([c_hist, c], axis=0).reshape(PAIR_HALO + PAIRS, 2, SUBLANES, D_MODEL)
    for l in range(LANE_BLOCKS):
        even, odd = _lane_block(c_pairs[:, 0], l), _lane_block(c_pairs[:, 1], l)
        x0_buf[l] = even
        x1_buf[l] = odd
        xs_buf[l] = even + odd

    shp = (SEG, SUBLANES, POOL_GROUP)
    pos = (s * TILE + lax.broadcasted_iota(jnp.int32, shp, 1) * SEG
           + lax.broadcasted_iota(jnp.int32, shp, 0))
    a_parts = []
    for g, w in enumerate(POOL_WINDOWS):
        zg = zp_buf[:, :, g * POOL_GROUP:(g + 1) * POOL_GROUP]
        ssum, span = zg, 1
        while span < w:
            ssum = ssum[span:] + ssum[:-span]
            span *= 2
        ssum = ssum[POOL_HALO - (w - 1):]
        count = jnp.minimum(pos + 1, w).astype(F32)
        pooled = ssum / count - zg[POOL_HALO:]
        a_parts.append(_dot(pooled.reshape(TILE, POOL_GROUP).astype(BF16), pw_ref[g]))
    a = jnp.concatenate(a_parts, axis=-1) * ps_ref[...]

    chunks = PAIRS // CONV_PAIRS

    def conv_pairs(i, carry):
        l = i // chunks
        m0 = (i % chunks) * CONV_PAIRS
        acc_a = acc_b = acc_c = None
        for e in range(EVEN_TAPS):
            t = x0_buf[l, pl.ds(m0 + PAIR_HALO - e, CONV_PAIRS)] * g_even[l, e][None]
            acc_a = t if acc_a is None else acc_a + t
        for e in range(ODD_TAPS):
            t = x1_buf[l, pl.ds(m0 + PAIR_HALO - 1 - e, CONV_PAIRS + 1)] * g_odd[l, e][None]
            acc_b = t if acc_b is None else acc_b + t
        y_buf[l, pl.ds(m0, CONV_PAIRS), 0] = acc_a + acc_b[:CONV_PAIRS]
        acc_ab = acc_a + acc_b[1:]
        for e in range(EVEN_TAPS):
            t = xs_buf[l, pl.ds(m0 + PAIR_HALO - e, CONV_PAIRS)] * g_sum[l, e][None]
            acc_c = t if acc_c is None else acc_c + t
        y_buf[l, pl.ds(m0, CONV_PAIRS), 1] = acc_c - acc_ab
        return carry

    lax.fori_loop(0, LANE_BLOCKS * chunks, conv_pairs, 0)
    g_pool = zcol(3)
    g_conv = zcol(4)
    y = jnp.concatenate([y_buf[l] for l in range(LANE_BLOCKS)], axis=-1).reshape(TILE, D_MODEL)
    y = y + cb_ref[...]
    mu = jnp.mean(y, axis=-1, keepdims=True)
    yc = y - mu
    var = jnp.mean(yc * yc, axis=-1, keepdims=True)
    yn = yc * lax.rsqrt(var + LN_EPS) * lg_ref[...] + lb_ref[...]
    cm = _dot((yn * jax.nn.sigmoid(yn)).astype(BF16), cwo_ref[...])

    m = jax.nn.sigmoid(g_pool) * a + jax.nn.sigmoid(g_conv) * cm
    o_ref[...] = h + _dot(m.astype(BF16), wo_ref[...])


def _mix_call(h, seq, nrm, win, pw, ps, cw, cb, lg, lb, cwo, wo):
    n_tok = h.shape[0]
    tiles_per_seq = seq // TILE
    tok_spec = pl.BlockSpec((TILE, D_MODEL), lambda b, s: (b * tiles_per_seq + s, 0))
    in_specs = [tok_spec, _const_spec((1, D_MODEL)), _const_spec((D_MODEL, win.shape[1])),
                _const_spec((N_POOL_GROUPS, POOL_GROUP, POOL_GROUP)), _const_spec((1, D_MODEL)),
                _const_spec((CONV_K, D_MODEL)), _const_spec((1, D_MODEL)),
                _const_spec((1, D_MODEL)), _const_spec((1, D_MODEL)),
                _const_spec((D_MODEL, D_MODEL)), _const_spec((D_MODEL, D_MODEL))]
    vreg_rows = lambda n: pltpu.VMEM((n, SUBLANES, D_MODEL), F32)
    slabs = lambda n: pltpu.VMEM((LANE_BLOCKS, n, SUBLANES, LANES), F32)
    return pl.pallas_call(
        _mix_kernel,
        grid=(n_tok // seq, tiles_per_seq),
        in_specs=in_specs,
        out_specs=tok_spec,
        out_shape=jax.ShapeDtypeStruct(h.shape, F32),
        scratch_shapes=[vreg_rows(POOL_HALO + SEG),
                        slabs(PAIR_HALO + PAIRS), slabs(PAIR_HALO + PAIRS),
                        slabs(PAIR_HALO + PAIRS),
                        pltpu.VMEM((LANE_BLOCKS, PAIRS, 2, SUBLANES, LANES), F32),
                        vreg_rows(POOL_HALO), vreg_rows(CONV_HALO),
                        slabs(EVEN_TAPS), slabs(ODD_TAPS), slabs(EVEN_TAPS)],
        compiler_params=pltpu.CompilerParams(
            dimension_semantics=("arbitrary", "arbitrary"), vmem_limit_bytes=VMEM_LIMIT),
        name="mixer",
    )(h, nrm, win, pw, ps, cw, cb, lg, lb, cwo, wo)


def kernel(x, p, ffn1_norm, ffn1_w_gate, ffn1_w_up, ffn1_w_down, mix_norm, w_in, pool_w, pool_scale, conv_dw_w, conv_dw_b, conv_ln_g, conv_ln_b, conv_w_out, w_out, ffn2_norm, ffn2_w_gate, ffn2_w_up, ffn2_w_down, ple_norm, ple_w_gate, ple_w_proj, final_norm):
    bsz, seq, d = x.shape
    depth = p.shape[0]
    n_tok = bsz * seq
    row = lambda v: v.reshape(1, -1)
    bf = lambda w: w.astype(BF16)
    h = x.reshape(n_tok, d)
    p_all = p.reshape(depth * n_tok, PLE_DIM)
    for i in range(depth):
        last = i == depth - 1
        h = _ffn_call(h, row(ffn1_norm[i]), bf(ffn1_w_gate[i]), bf(ffn1_w_up[i]),
                      bf(ffn1_w_down[i]), to_segment=(i == 0))
        h = _mix_call(h, seq, row(mix_norm[i]), bf(w_in[i]), bf(pool_w[i]),
                      row(pool_scale[i]), conv_dw_w[i], row(conv_dw_b[i]), row(conv_ln_g[i]),
                      row(conv_ln_b[i]), bf(conv_w_out[i]), bf(w_out[i]))
        ple_args = (p_all, i, row(ple_norm[i]), bf(ple_w_gate[i]), bf(ple_w_proj[i]))
        h = _ffn_call(h, row(ffn2_norm[i]), bf(ffn2_w_gate[i]), bf(ffn2_w_up[i]),
                      bf(ffn2_w_down[i]), ple_args=ple_args,
                      final_norm=row(final_norm) if last else None, to_natural=last)
    return h.reshape(bsz, seq, d)
```

```python
import functools

import jax
import jax.numpy as jnp
from jax import lax
from jax.experimental import pallas as pl
from jax.experimental.pallas import tpu as pltpu

D_MODEL = 1024
PLE_DIM = 256
D_FF = 2816
N_POOL_GROUPS = 4
POOL_GROUP = D_MODEL // N_POOL_GROUPS
POOL_WINDOWS = (2, 4, 8, 16)
CONV_K = 31
RMS_EPS = 1e-6
LN_EPS = 1e-5

F32 = jnp.float32
BF16 = jnp.bfloat16

SUBLANES = 8
LANES = 128
TILE = 512
SEG = TILE // SUBLANES
LANE_BLOCKS = D_MODEL // LANES
FF_CHUNK = 256
FF_CHUNKS = tuple((c0, FF_CHUNK) for c0 in range(0, D_FF, FF_CHUNK))
POOL_HALO = 16
CONV_HALO = 32
VMEM_LIMIT = 56 * 1024 * 1024


def _dot(a, b):
    return jnp.dot(a, b, preferred_element_type=F32)


def _rms(x, g):
    ms = jnp.mean(x * x, axis=-1, keepdims=True)
    return x * lax.rsqrt(ms + RMS_EPS) * g


def _const_spec(shape):
    nd = len(shape)
    return pl.BlockSpec(shape, lambda *_: (0,) * nd, pipeline_mode=pl.Buffered(1))


def _lane_block(v, l):
    return v[..., l * LANES:(l + 1) * LANES]


def _interleaved_start(r):
    rows_per_seg = SEG // SUBLANES
    return SUBLANES * SUBLANES * (r % rows_per_seg) + r // rows_per_seg


def _to_segment_order(val, slab):
    nblk = val.shape[1] // LANES
    for l in range(nblk):
        for r in range(TILE // SUBLANES):
            rows = pl.ds(_interleaved_start(r), SUBLANES, stride=SUBLANES)
            slab[l, rows, :] = _lane_block(val[r * SUBLANES:(r + 1) * SUBLANES], l)
    return jnp.concatenate([slab[l] for l in range(nblk)], axis=-1)


def _to_natural_order(val, slab):
    nblk = val.shape[1] // LANES
    for l in range(nblk):
        slab[l] = _lane_block(val, l)
    out_rows = []
    for r in range(TILE // SUBLANES):
        rows = pl.ds(_interleaved_start(r), SUBLANES, stride=SUBLANES)
        out_rows.append(jnp.concatenate([slab[l, rows, :] for l in range(nblk)], axis=-1))
    return jnp.concatenate(out_rows, axis=0)


def _ffn_kernel(*refs, ple, final, to_segment, to_natural):
    h_ref, g_ref, wg_ref, wu_ref, wd_ref = refs[:5]
    n_in = 5 + (4 if ple else 0) + (1 if final else 0)
    o_ref = refs[n_in]
    slab = refs[n_in + 1] if len(refs) > n_in + 1 else None
    h = h_ref[...]
    xn = _rms(h, g_ref[...]).astype(BF16)
    acc = None
    for c0, cw in FF_CHUNKS:
        g = _dot(xn, wg_ref[:, c0:c0 + cw])
        u = _dot(xn, wu_ref[:, c0:c0 + cw])
        a = (g * jax.nn.sigmoid(g) * u).astype(BF16)
        d = _dot(a, wd_ref[c0:c0 + cw, :])
        acc = d if acc is None else acc + d
    out = h + 0.5 * acc
    if ple:
        p_ref, pn_ref, pwg_ref, pwp_ref = refs[5:9]
        p_tile = _to_segment_order(p_ref[...], slab)
        hn = _rms(out, pn_ref[...]).astype(BF16)
        gate = jax.nn.sigmoid(_dot(hn, pwg_ref[...]))
        pe = _dot(p_tile.astype(BF16), pwp_ref[...])
        out = out + gate * pe
    if final:
        fn_ref = refs[n_in - 1]
        out = _rms(out, fn_ref[...])
    if to_segment:
        out = _to_segment_order(out, slab)
    if to_natural:
        out = _to_natural_order(out, slab)
    o_ref[...] = out


def _ffn_call(h, norm, wg, wu, wd, ple_args=None, final_norm=None, to_segment=False,
              to_natural=False):
    n_tok = h.shape[0]
    n_tiles = n_tok // TILE
    tok_spec = pl.BlockSpec((TILE, D_MODEL), lambda i: (i, 0))
    in_specs = [tok_spec, _const_spec((1, D_MODEL)), _const_spec((D_MODEL, D_FF)),
                _const_spec((D_MODEL, D_FF)), _const_spec((D_FF, D_MODEL))]
    args = [h, norm, wg, wu, wd]
    if ple_args is not None:
        p_all, layer, pn, pwg, pwp = ple_args
        in_specs += [pl.BlockSpec((TILE, PLE_DIM), lambda i: (layer * n_tiles + i, 0)),
                     _const_spec((1, D_MODEL)), _const_spec((D_MODEL, D_MODEL)),
                     _const_spec((PLE_DIM, D_MODEL))]
        args += [p_all, pn, pwg, pwp]
    if final_norm is not None:
        in_specs.append(_const_spec((1, D_MODEL)))
        args.append(final_norm)
    needs_slab = ple_args is not None or to_segment or to_natural
    scratch = [pltpu.VMEM((LANE_BLOCKS, TILE, LANES), F32)] if needs_slab else []
    return pl.pallas_call(
        functools.partial(_ffn_kernel, ple=ple_args is not None, final=final_norm is not None,
                          to_segment=to_segment, to_natural=to_natural),
        grid=(n_tiles,),
        in_specs=in_specs,
        out_specs=tok_spec,
        out_shape=jax.ShapeDtypeStruct((n_tok, D_MODEL), F32),
        scratch_shapes=scratch,
        compiler_params=pltpu.CompilerParams(
            dimension_semantics=("arbitrary",), vmem_limit_bytes=VMEM_LIMIT),
        name="ffn_ple" if ple_args is not None else "ffn",
    )(*args)


QUAD_HALO = CONV_HALO // 4
QUADS = SEG // 4
QUAD_TAPS = (CONV_K + 3) // 4
N_STREAMS = 9


def _with_history(cur, carry_ref, halo):
    rolled = pltpu.roll(cur[SEG - halo:], 1, axis=1)
    sub = lax.broadcasted_iota(jnp.int32, rolled.shape, 1)
    hist = jnp.where(sub == 0, carry_ref[...], rolled)
    carry_ref[...] = rolled
    return hist


def _mix_kernel(h_ref, nrm_ref, win_ref, pw_ref, ps_ref, cw_ref, cb_ref, lg_ref, lb_ref,
                cwo_ref, wo_ref, o_ref, zp_buf, s_buf, y_buf, ab_buf, zp_carry, c_carry, g_buf):
    s = pl.program_id(1)

    @pl.when(s == 0)
    def _():
        zp_carry[...] = jnp.zeros(zp_carry.shape, F32)
        c_carry[...] = jnp.zeros(c_carry.shape, F32)

        def delay_tap(d):
            if d >= CONV_K:
                return jnp.zeros((SUBLANES, D_MODEL), F32)
            return jnp.broadcast_to(cw_ref[CONV_K - 1 - d:CONV_K - d, :], (SUBLANES, D_MODEL))

        g0 = [delay_tap(2 * e) for e in range(2 * QUAD_TAPS)]
        g1 = [delay_tap(2 * e + 1) for e in range(2 * QUAD_TAPS)]
        tables = []
        for half in (g0, g1, [a + b for a, b in zip(g0, g1)]):
            even, odd = half[0::2], half[1::2]
            tables += [even, odd, [a + b for a, b in zip(even, odd)]]
        for wi, table in enumerate(tables):
            for t in range(QUAD_TAPS):
                for l in range(LANE_BLOCKS):
                    g_buf[wi, l, t] = _lane_block(table[t], l)

    h = h_ref[...]
    u = _rms(h, nrm_ref[...]).astype(BF16)

    def zcol(i):
        return _dot(u, win_ref[:, i * D_MODEL:(i + 1) * D_MODEL])

    zp = zcol(0).reshape(SEG, SUBLANES, D_MODEL)
    zp_buf[0:POOL_HALO] = _with_history(zp, zp_carry, POOL_HALO)
    zp_buf[POOL_HALO:] = zp
    c = (zcol(1) * jax.nn.sigmoid(zcol(2))).reshape(SEG, SUBLANES, D_MODEL)
    c_hist = _with_history(c, c_carry, CONV_HALO)
    c_quads = jnp.concatenate([c_hist, c], axis=0).reshape(QUAD_HALO + QUADS, 4, SUBLANES, D_MODEL)
    c0, c1, c2, c3 = (c_quads[:, i] for i in range(4))
    s01, s23 = c0 + c1, c2 + c3
    streams = (c0, c2, c0 + c2, c1, c3, c1 + c3, s01, s23, s01 + s23)
    for si, stream in enumerate(streams):
        for l in range(LANE_BLOCKS):
            s_buf[si, l] = _lane_block(stream, l)

    shp = (SEG, SUBLANES, POOL_GROUP)
    pos = (s * TILE + lax.broadcasted_iota(jnp.int32, shp, 1) * SEG
           + lax.broadcasted_iota(jnp.int32, shp, 0))
    a_parts = []
    for g, w in enumerate(POOL_WINDOWS):
        zg = zp_buf[:, :, g * POOL_GROUP:(g + 1) * POOL_GROUP]
        ssum, span = zg, 1
        while span < w:
            ssum = ssum[span:] + ssum[:-span]
            span *= 2
        ssum = ssum[POOL_HALO - (w - 1):]
        count = jnp.minimum(pos + 1, w).astype(F32)
        pooled = ssum / count - zg[POOL_HALO:]
        a_parts.append(_dot(pooled.reshape(TILE, POOL_GROUP).astype(BF16), pw_ref[g]))
    a = jnp.concatenate(a_parts, axis=-1) * ps_ref[...]

    n = QUADS

    def conv_lane_block(l, carry):
        def filt(si, first, count, taps=QUAD_TAPS):
            acc = None
            for t in range(taps):
                v = s_buf[si, l, pl.ds(first + QUAD_HALO - t, count)] * g_buf[si, l, t][None]
                acc = v if acc is None else acc + v
            return acc

        a, b = filt(0, 0, n), filt(1, -1, n + 1)
        ab_buf[0, 0:n] = a + b[:n]
        ab_buf[1, 0:n] = filt(2, 0, n) - (a + b[1:])
        a, b = filt(3, -1, n + 1), filt(4, -2, n + 2, taps=QUAD_TAPS - 1)
        ab_buf[2] = a + b[:n + 1]
        ab_buf[3] = filt(5, -1, n + 1) - (a + b[1:])
        a, b = filt(6, 0, n), filt(7, -1, n + 1)
        c_even, c_rest = a + b[:n], a + b[1:]
        a_even, a_odd = ab_buf[0, 0:n], ab_buf[1, 0:n]
        b_even, b_odd = ab_buf[2], ab_buf[3]
        y_buf[l, :, 0] = a_even + b_odd[:n]
        y_buf[l, :, 1] = c_even - a_even - b_even[1:]
        y_buf[l, :, 2] = a_odd + b_even[1:]
        y_buf[l, :, 3] = (filt(8, 0, n) - c_rest) - a_odd - b_odd[1:]
        return carry

    lax.fori_loop(0, LANE_BLOCKS, conv_lane_block, 0)
    g_pool = zcol(3)
    g_conv = zcol(4)
    y = jnp.concatenate([y_buf[l] for l in range(LANE_BLOCKS)], axis=-1).reshape(TILE, D_MODEL)
    y = y + cb_ref[...]
    mu = jnp.mean(y, axis=-1, keepdims=True)
    yc = y - mu
    var = jnp.mean(yc * yc, axis=-1, keepdims=True)
    yn = yc * lax.rsqrt(var + LN_EPS) * lg_ref[...] + lb_ref[...]
    cm = _dot((yn * jax.nn.sigmoid(yn)).astype(BF16), cwo_ref[...])

    m = jax.nn.sigmoid(g_pool) * a + jax.nn.sigmoid(g_conv) * cm
    o_ref[...] = h + _dot(m.astype(BF16), wo_ref[...])


def _mix_call(h, seq, nrm, win, pw, ps, cw, cb, lg, lb, cwo, wo):
    n_tok = h.shape[0]
    tiles_per_seq = seq // TILE
    tok_spec = pl.BlockSpec((TILE, D_MODEL), lambda b, s: (b * tiles_per_seq + s, 0))
    in_specs = [tok_spec, _const_spec((1, D_MODEL)), _const_spec((D_MODEL, win.shape[1])),
                _const_spec((N_POOL_GROUPS, POOL_GROUP, POOL_GROUP)), _const_spec((1, D_MODEL)),
                _const_spec((CONV_K, D_MODEL)), _const_spec((1, D_MODEL)),
                _const_spec((1, D_MODEL)), _const_spec((1, D_MODEL)),
                _const_spec((D_MODEL, D_MODEL)), _const_spec((D_MODEL, D_MODEL))]
    vreg_rows = lambda n: pltpu.VMEM((n, SUBLANES, D_MODEL), F32)
    slabs = lambda k, n: pltpu.VMEM((k, LANE_BLOCKS, n, SUBLANES, LANES), F32)
    return pl.pallas_call(
        _mix_kernel,
        grid=(n_tok // seq, tiles_per_seq),
        in_specs=in_specs,
        out_specs=tok_spec,
        out_shape=jax.ShapeDtypeStruct(h.shape, F32),
        scratch_shapes=[vreg_rows(POOL_HALO + SEG),
                        slabs(N_STREAMS, QUAD_HALO + QUADS),
                        pltpu.VMEM((LANE_BLOCKS, QUADS, 4, SUBLANES, LANES), F32),
                        pltpu.VMEM((4, QUADS + 1, SUBLANES, LANES), F32),
                        vreg_rows(POOL_HALO), vreg_rows(CONV_HALO),
                        slabs(N_STREAMS, QUAD_TAPS)],
        compiler_params=pltpu.CompilerParams(
            dimension_semantics=("arbitrary", "arbitrary"), vmem_limit_bytes=VMEM_LIMIT),
        name="mixer",
    )(h, nrm, win, pw, ps, cw, cb, lg, lb, cwo, wo)


def kernel(x, p, ffn1_norm, ffn1_w_gate, ffn1_w_up, ffn1_w_down, mix_norm, w_in, pool_w, pool_scale, conv_dw_w, conv_dw_b, conv_ln_g, conv_ln_b, conv_w_out, w_out, ffn2_norm, ffn2_w_gate, ffn2_w_up, ffn2_w_down, ple_norm, ple_w_gate, ple_w_proj, final_norm):
    bsz, seq, d = x.shape
    depth = p.shape[0]
    n_tok = bsz * seq
    row = lambda v: v.reshape(1, -1)
    bf = lambda w: w.astype(BF16)
    h = x.reshape(n_tok, d)
    p_all = p.reshape(depth * n_tok, PLE_DIM)
    for i in range(depth):
        last = i == depth - 1
        h = _ffn_call(h, row(ffn1_norm[i]), bf(ffn1_w_gate[i]), bf(ffn1_w_up[i]),
                      bf(ffn1_w_down[i]), to_segment=(i == 0))
        h = _mix_call(h, seq, row(mix_norm[i]), bf(w_in[i]), bf(pool_w[i]),
                      row(pool_scale[i]), conv_dw_w[i], row(conv_dw_b[i]), row(conv_ln_g[i]),
                      row(conv_ln_b[i]), bf(conv_w_out[i]), bf(w_out[i]))
        ple_args = (p_all, i, row(ple_norm[i]), bf(ple_w_gate[i]), bf(ple_w_proj[i]))
        h = _ffn_call(h, row(ffn2_norm[i]), bf(ffn2_w_gate[i]), bf(ffn2_w_up[i]),
                      bf(ffn2_w_down[i]), ple_args=ple_args,
                      final_norm=row(final_norm) if last else None, to_natural=last)
    return h.reshape(bsz, seq, d)
```

```python
import functools

import jax
import jax.numpy as jnp
from jax import lax
from jax.experimental import pallas as pl
from jax.experimental.pallas import tpu as pltpu

D_MODEL = 1024
PLE_DIM = 256
D_FF = 2816
N_POOL_GROUPS = 4
POOL_GROUP = D_MODEL // N_POOL_GROUPS
POOL_WINDOWS = (2, 4, 8, 16)
CONV_K = 31
RMS_EPS = 1e-6
LN_EPS = 1e-5

F32 = jnp.float32
BF16 = jnp.bfloat16

SUBLANES = 8
LANES = 128
TILE = 512
SEG = TILE // SUBLANES
LANE_BLOCKS = D_MODEL // LANES
FF_CHUNK = 256
FF_CHUNKS = tuple((c0, FF_CHUNK) for c0 in range(0, D_FF, FF_CHUNK))
POOL_HALO = 16
CONV_HALO = 32
VMEM_LIMIT = 56 * 1024 * 1024


def _dot(a, b):
    return jnp.dot(a, b, preferred_element_type=F32)


def _rms(x, g):
    ms = jnp.mean(x * x, axis=-1, keepdims=True)
    return x * lax.rsqrt(ms + RMS_EPS) * g


def _const_spec(shape, layer=None):
    nd = len(shape)
    if layer is None:
        return pl.BlockSpec(shape, lambda *_: (0,) * nd, pipeline_mode=pl.Buffered(1))
    return pl.BlockSpec((None,) + tuple(shape), lambda *_: (layer,) + (0,) * nd,
                        pipeline_mode=pl.Buffered(1))


def _lane_block(v, l):
    return v[..., l * LANES:(l + 1) * LANES]


def _interleaved_start(r):
    rows_per_seg = SEG // SUBLANES
    return SUBLANES * SUBLANES * (r % rows_per_seg) + r // rows_per_seg


def _to_segment_order(val, slab):
    nblk = val.shape[1] // LANES
    for l in range(nblk):
        for r in range(TILE // SUBLANES):
            rows = pl.ds(_interleaved_start(r), SUBLANES, stride=SUBLANES)
            slab[l, rows, :] = _lane_block(val[r * SUBLANES:(r + 1) * SUBLANES], l)
    return jnp.concatenate([slab[l] for l in range(nblk)], axis=-1)


def _to_natural_order(val, slab):
    nblk = val.shape[1] // LANES
    for l in range(nblk):
        slab[l] = _lane_block(val, l)
    out_rows = []
    for r in range(TILE // SUBLANES):
        rows = pl.ds(_interleaved_start(r), SUBLANES, stride=SUBLANES)
        out_rows.append(jnp.concatenate([slab[l, rows, :] for l in range(nblk)], axis=-1))
    return jnp.concatenate(out_rows, axis=0)


def _ffn_kernel(*refs, ple, final, to_segment, to_natural):
    h_ref, g_ref, wg_ref, wu_ref, wd_ref = refs[:5]
    n_in = 5 + (4 if ple else 0) + (1 if final else 0)
    o_ref = refs[n_in]
    slab = refs[n_in + 1] if len(refs) > n_in + 1 else None
    h = h_ref[...]
    xn = _rms(h, g_ref[...]).astype(BF16)
    acc = None
    for c0, cw in FF_CHUNKS:
        g = _dot(xn, wg_ref[:, c0:c0 + cw])
        u = _dot(xn, wu_ref[:, c0:c0 + cw])
        a = (g * jax.nn.sigmoid(g) * u).astype(BF16)
        d = _dot(a, wd_ref[c0:c0 + cw, :])
        acc = d if acc is None else acc + d
    out = h + 0.5 * acc
    if ple:
        p_ref, pn_ref, pwg_ref, pwp_ref = refs[5:9]
        p_tile = _to_segment_order(p_ref[...], slab)
        hn = _rms(out, pn_ref[...]).astype(BF16)
        gate = jax.nn.sigmoid(_dot(hn, pwg_ref[...]))
        pe = _dot(p_tile.astype(BF16), pwp_ref[...])
        out = out + gate * pe
    if final:
        fn_ref = refs[n_in - 1]
        out = _rms(out, fn_ref[...])
    if to_segment:
        out = _to_segment_order(out, slab)
    if to_natural:
        out = _to_natural_order(out, slab)
    o_ref[...] = out


def _ffn_call(h, layer, norm, wg, wu, wd, ple_args=None, final_norm=None, to_segment=False,
              to_natural=False):
    n_tok = h.shape[0]
    n_tiles = n_tok // TILE
    tok_spec = pl.BlockSpec((TILE, D_MODEL), lambda i: (i, 0))
    in_specs = [tok_spec, _const_spec((1, D_MODEL), layer), _const_spec((D_MODEL, D_FF), layer),
                _const_spec((D_MODEL, D_FF), layer), _const_spec((D_FF, D_MODEL), layer)]
    args = [h, norm, wg, wu, wd]
    if ple_args is not None:
        p_all, pn, pwg, pwp = ple_args
        in_specs += [pl.BlockSpec((TILE, PLE_DIM), lambda i: (layer * n_tiles + i, 0)),
                     _const_spec((1, D_MODEL), layer), _const_spec((D_MODEL, D_MODEL), layer),
                     _const_spec((PLE_DIM, D_MODEL), layer)]
        args += [p_all, pn, pwg, pwp]
    if final_norm is not None:
        in_specs.append(_const_spec((1, D_MODEL)))
        args.append(final_norm)
    needs_slab = ple_args is not None or to_segment or to_natural
    scratch = [pltpu.VMEM((LANE_BLOCKS, TILE, LANES), F32)] if needs_slab else []
    return pl.pallas_call(
        functools.partial(_ffn_kernel, ple=ple_args is not None, final=final_norm is not None,
                          to_segment=to_segment, to_natural=to_natural),
        grid=(n_tiles,),
        in_specs=in_specs,
        out_specs=tok_spec,
        out_shape=jax.ShapeDtypeStruct((n_tok, D_MODEL), F32),
        scratch_shapes=scratch,
        compiler_params=pltpu.CompilerParams(
            dimension_semantics=("arbitrary",), vmem_limit_bytes=VMEM_LIMIT),
        name="ffn_ple" if ple_args is not None else "ffn",
    )(*args)


QUAD_HALO = CONV_HALO // 4
QUADS = SEG // 4
QUAD_TAPS = (CONV_K + 3) // 4
N_STREAMS = 9


def _with_history(cur, carry_ref, halo):
    rolled = pltpu.roll(cur[SEG - halo:], 1, axis=1)
    sub = lax.broadcasted_iota(jnp.int32, rolled.shape, 1)
    hist = jnp.where(sub == 0, carry_ref[...], rolled)
    carry_ref[...] = rolled
    return hist


def _mix_kernel(h_ref, nrm_ref, win_ref, pw_ref, ps_ref, cw_ref, cb_ref, lg_ref, lb_ref,
                cwo_ref, wo_ref, o_ref, zp_buf, s_buf, y_buf, ab_buf, zp_carry, c_carry, g_buf):
    s = pl.program_id(1)

    @pl.when(s == 0)
    def _():
        zp_carry[...] = jnp.zeros(zp_carry.shape, F32)
        c_carry[...] = jnp.zeros(c_carry.shape, F32)

        def delay_tap(d):
            if d >= CONV_K:
                return jnp.zeros((SUBLANES, D_MODEL), F32)
            return jnp.broadcast_to(cw_ref[CONV_K - 1 - d:CONV_K - d, :], (SUBLANES, D_MODEL))

        g0 = [delay_tap(2 * e) for e in range(2 * QUAD_TAPS)]
        g1 = [delay_tap(2 * e + 1) for e in range(2 * QUAD_TAPS)]
        tables = []
        for half in (g0, g1, [a + b for a, b in zip(g0, g1)]):
            even, odd = half[0::2], half[1::2]
            tables += [even, odd, [a + b for a, b in zip(even, odd)]]
        for wi, table in enumerate(tables):
            for t in range(QUAD_TAPS):
                for l in range(LANE_BLOCKS):
                    g_buf[wi, l, t] = _lane_block(table[t], l)

    h = h_ref[...]
    u = _rms(h, nrm_ref[...]).astype(BF16)

    def zcol(i):
        return _dot(u, win_ref[:, i * D_MODEL:(i + 1) * D_MODEL])

    zp = zcol(0).reshape(SEG, SUBLANES, D_MODEL)
    zp_buf[0:POOL_HALO] = _with_history(zp, zp_carry, POOL_HALO)
    zp_buf[POOL_HALO:] = zp
    c = (zcol(1) * jax.nn.sigmoid(zcol(2))).reshape(SEG, SUBLANES, D_MODEL)
    c_hist = _with_history(c, c_carry, CONV_HALO)
    c_quads = jnp.concatenate([c_hist, c], axis=0).reshape(QUAD_HALO + QUADS, 4, SUBLANES, D_MODEL)
    c0, c1, c2, c3 = (c_quads[:, i] for i in range(4))
    s01, s23 = c0 + c1, c2 + c3
    streams = (c0, c2, c0 + c2, c1, c3, c1 + c3, s01, s23, s01 + s23)
    for si, stream in enumerate(streams):
        for l in range(LANE_BLOCKS):
            s_buf[si, l] = _lane_block(stream, l)

    shp = (SEG, SUBLANES, POOL_GROUP)
    pos = (s * TILE + lax.broadcasted_iota(jnp.int32, shp, 1) * SEG
           + lax.broadcasted_iota(jnp.int32, shp, 0))
    a_parts = []
    for g, w in enumerate(POOL_WINDOWS):
        zg = zp_buf[:, :, g * POOL_GROUP:(g + 1) * POOL_GROUP]
        ssum, span = zg, 1
        while span < w:
            ssum = ssum[span:] + ssum[:-span]
            span *= 2
        ssum = ssum[POOL_HALO - (w - 1):]
        count = jnp.minimum(pos + 1, w).astype(F32)
        pooled = ssum / count - zg[POOL_HALO:]
        a_parts.append(_dot(pooled.reshape(TILE, POOL_GROUP).astype(BF16), pw_ref[g]))
    a = jnp.concatenate(a_parts, axis=-1) * ps_ref[...]

    n = QUADS

    def conv_lane_block(l, carry):
        def filt(si, first, count, taps=QUAD_TAPS):
            acc = None
            for t in range(taps):
                v = s_buf[si, l, pl.ds(first + QUAD_HALO - t, count)] * g_buf[si, l, t][None]
                acc = v if acc is None else acc + v
            return acc

        a, b = filt(0, 0, n), filt(1, -1, n + 1)
        ab_buf[0, 0:n] = a + b[:n]
        ab_buf[1, 0:n] = filt(2, 0, n) - (a + b[1:])
        a, b = filt(3, -1, n + 1), filt(4, -2, n + 2, taps=QUAD_TAPS - 1)
        ab_buf[2] = a + b[:n + 1]
        ab_buf[3] = filt(5, -1, n + 1) - (a + b[1:])
        a, b = filt(6, 0, n), filt(7, -1, n + 1)
        c_even, c_rest = a + b[:n], a + b[1:]
        a_even, a_odd = ab_buf[0, 0:n], ab_buf[1, 0:n]
        b_even, b_odd = ab_buf[2], ab_buf[3]
        y_buf[l, :, 0] = a_even + b_odd[:n]
        y_buf[l, :, 1] = c_even - a_even - b_even[1:]
        y_buf[l, :, 2] = a_odd + b_even[1:]
        y_buf[l, :, 3] = (filt(8, 0, n) - c_rest) - a_odd - b_odd[1:]
        return carry

    lax.fori_loop(0, LANE_BLOCKS, conv_lane_block, 0)
    g_pool = zcol(3)
    g_conv = zcol(4)
    y = jnp.concatenate([y_buf[l] for l in range(LANE_BLOCKS)], axis=-1).reshape(TILE, D_MODEL)
    y = y + cb_ref[...]
    mu = jnp.mean(y, axis=-1, keepdims=True)
    yc = y - mu
    var = jnp.mean(yc * yc, axis=-1, keepdims=True)
    yn = yc * lax.rsqrt(var + LN_EPS) * lg_ref[...] + lb_ref[...]
    cm = _dot((yn * jax.nn.sigmoid(yn)).astype(BF16), cwo_ref[...])

    m = jax.nn.sigmoid(g_pool) * a + jax.nn.sigmoid(g_conv) * cm
    o_ref[...] = h + _dot(m.astype(BF16), wo_ref[...])


def _mix_call(h, seq, layer, nrm, win, pw, ps, cw, cb, lg, lb, cwo, wo):
    n_tok = h.shape[0]
    tiles_per_seq = seq // TILE
    tok_spec = pl.BlockSpec((TILE, D_MODEL), lambda b, s: (b * tiles_per_seq + s, 0))
    vec = _const_spec((1, D_MODEL), layer)
    mat = _const_spec((D_MODEL, D_MODEL), layer)
    in_specs = [tok_spec, vec, _const_spec((D_MODEL, win.shape[2]), layer),
                _const_spec((N_POOL_GROUPS, POOL_GROUP, POOL_GROUP), layer), vec,
                _const_spec((CONV_K, D_MODEL), layer), vec, vec, vec, mat, mat]
    vreg_rows = lambda n: pltpu.VMEM((n, SUBLANES, D_MODEL), F32)
    slabs = lambda k, n: pltpu.VMEM((k, LANE_BLOCKS, n, SUBLANES, LANES), F32)
    return pl.pallas_call(
        _mix_kernel,
        grid=(n_tok // seq, tiles_per_seq),
        in_specs=in_specs,
        out_specs=tok_spec,
        out_shape=jax.ShapeDtypeStruct(h.shape, F32),
        scratch_shapes=[vreg_rows(POOL_HALO + SEG),
                        slabs(N_STREAMS, QUAD_HALO + QUADS),
                        pltpu.VMEM((LANE_BLOCKS, QUADS, 4, SUBLANES, LANES), F32),
                        pltpu.VMEM((4, QUADS + 1, SUBLANES, LANES), F32),
                        vreg_rows(POOL_HALO), vreg_rows(CONV_HALO),
                        slabs(N_STREAMS, QUAD_TAPS)],
        compiler_params=pltpu.CompilerParams(
            dimension_semantics=("arbitrary", "arbitrary"), vmem_limit_bytes=VMEM_LIMIT),
        name="mixer",
    )(h, nrm, win, pw, ps, cw, cb, lg, lb, cwo, wo)


def kernel(x, p, ffn1_norm, ffn1_w_gate, ffn1_w_up, ffn1_w_down, mix_norm, w_in, pool_w, pool_scale, conv_dw_w, conv_dw_b, conv_ln_g, conv_ln_b, conv_w_out, w_out, ffn2_norm, ffn2_w_gate, ffn2_w_up, ffn2_w_down, ple_norm, ple_w_gate, ple_w_proj, final_norm):
    bsz, seq, d = x.shape
    depth = p.shape[0]
    n_tok = bsz * seq
    rows = lambda v: v.reshape(v.shape[0], 1, -1)
    bf = lambda w: w.astype(BF16)
    h = x.reshape(n_tok, d)
    p_all = p.reshape(depth * n_tok, PLE_DIM)
    ffn1 = (rows(ffn1_norm), bf(ffn1_w_gate), bf(ffn1_w_up), bf(ffn1_w_down))
    ffn2 = (rows(ffn2_norm), bf(ffn2_w_gate), bf(ffn2_w_up), bf(ffn2_w_down))
    mix = (rows(mix_norm), bf(w_in), bf(pool_w), rows(pool_scale), conv_dw_w, rows(conv_dw_b),
           rows(conv_ln_g), rows(conv_ln_b), bf(conv_w_out), bf(w_out))
    ple_args = (p_all, rows(ple_norm), bf(ple_w_gate), bf(ple_w_proj))
    for i in range(depth):
        last = i == depth - 1
        h = _ffn_call(h, i, *ffn1, to_segment=(i == 0))
        h = _mix_call(h, seq, i, *mix)
        h = _ffn_call(h, i, *ffn2, ple_args=ple_args,
                      final_norm=final_norm.reshape(1, -1) if last else None, to_natural=last)
    return h.reshape(bsz, seq, d)
```

```python
import functools

import jax
import jax.numpy as jnp
from jax import lax
from jax.experimental import pallas as pl
from jax.experimental.pallas import tpu as pltpu

D_MODEL = 1024
PLE_DIM = 256
D_FF = 2816
N_POOL_GROUPS = 4
POOL_GROUP = D_MODEL // N_POOL_GROUPS
POOL_WINDOWS = (2, 4, 8, 16)
CONV_K = 31
RMS_EPS = 1e-6
LN_EPS = 1e-5

F32 = jnp.float32
BF16 = jnp.bfloat16

SUBLANES = 8
LANES = 128
TILE = 512
SEG = TILE // SUBLANES
LANE_BLOCKS = D_MODEL // LANES
FF_CHUNK = 256
FF_CHUNKS = tuple((c0, FF_CHUNK) for c0 in range(0, D_FF, FF_CHUNK))
POOL_HALO = 16
CONV_HALO = 32
VMEM_LIMIT = 56 * 1024 * 1024


def _dot(a, b):
    return jnp.dot(a, b, preferred_element_type=F32)


def _rms(x, g):
    ms = jnp.mean(x * x, axis=-1, keepdims=True)
    return x * lax.rsqrt(ms + RMS_EPS) * g


def _const_spec(shape, layer=None):
    nd = len(shape)
    if layer is None:
        return pl.BlockSpec(shape, lambda *_: (0,) * nd, pipeline_mode=pl.Buffered(1))
    return pl.BlockSpec((None,) + tuple(shape), lambda *_: (layer,) + (0,) * nd,
                        pipeline_mode=pl.Buffered(1))


def _lane_block(v, l):
    return v[..., l * LANES:(l + 1) * LANES]


def _interleaved_start(r):
    rows_per_seg = SEG // SUBLANES
    return SUBLANES * SUBLANES * (r % rows_per_seg) + r // rows_per_seg


def _to_segment_order(val, slab):
    nblk = val.shape[1] // LANES
    for l in range(nblk):
        for r in range(TILE // SUBLANES):
            rows = pl.ds(_interleaved_start(r), SUBLANES, stride=SUBLANES)
            slab[l, rows, :] = _lane_block(val[r * SUBLANES:(r + 1) * SUBLANES], l)
    return jnp.concatenate([slab[l] for l in range(nblk)], axis=-1)


def _to_natural_order(val, slab):
    nblk = val.shape[1] // LANES
    for l in range(nblk):
        slab[l] = _lane_block(val, l)
    out_rows = []
    for r in range(TILE // SUBLANES):
        rows = pl.ds(_interleaved_start(r), SUBLANES, stride=SUBLANES)
        out_rows.append(jnp.concatenate([slab[l, rows, :] for l in range(nblk)], axis=-1))
    return jnp.concatenate(out_rows, axis=0)


def _ffn_kernel(*refs, ple, final, to_segment, to_natural):
    h_ref, g_ref, wg_ref, wu_ref, wd_ref = refs[:5]
    n_in = 5 + (4 if ple else 0) + (1 if final else 0)
    o_ref = refs[n_in]
    slab = refs[n_in + 1] if len(refs) > n_in + 1 else None
    if ple:
        p_ref, pn_ref, pwg_ref, pwp_ref = refs[5:9]
        pe = _dot(_to_segment_order(p_ref[...], slab).astype(BF16), pwp_ref[...])
    h = h_ref[...]
    xn = _rms(h, g_ref[...]).astype(BF16)
    acc = None
    for c0, cw in FF_CHUNKS:
        g = _dot(xn, wg_ref[:, c0:c0 + cw])
        u = _dot(xn, wu_ref[:, c0:c0 + cw])
        a = (g * jax.nn.sigmoid(g) * u).astype(BF16)
        d = _dot(a, wd_ref[c0:c0 + cw, :])
        acc = d if acc is None else acc + d
    out = h + 0.5 * acc
    if ple:
        hn = _rms(out, pn_ref[...]).astype(BF16)
        gate = jax.nn.sigmoid(_dot(hn, pwg_ref[...]))
        out = out + gate * pe
    if final:
        fn_ref = refs[n_in - 1]
        out = _rms(out, fn_ref[...])
    if to_segment:
        out = _to_segment_order(out, slab)
    if to_natural:
        out = _to_natural_order(out, slab)
    o_ref[...] = out


def _ffn_call(h, layer, norm, wg, wu, wd, ple_args=None, final_norm=None, to_segment=False,
              to_natural=False):
    n_tok = h.shape[0]
    n_tiles = n_tok // TILE
    tok_spec = pl.BlockSpec((TILE, D_MODEL), lambda i: (i, 0))
    in_specs = [tok_spec, _const_spec((1, D_MODEL), layer), _const_spec((D_MODEL, D_FF), layer),
                _const_spec((D_MODEL, D_FF), layer), _const_spec((D_FF, D_MODEL), layer)]
    args = [h, norm, wg, wu, wd]
    if ple_args is not None:
        p_all, pn, pwg, pwp = ple_args
        in_specs += [pl.BlockSpec((TILE, PLE_DIM), lambda i: (layer * n_tiles + i, 0)),
                     _const_spec((1, D_MODEL), layer), _const_spec((D_MODEL, D_MODEL), layer),
                     _const_spec((PLE_DIM, D_MODEL), layer)]
        args += [p_all, pn, pwg, pwp]
    if final_norm is not None:
        in_specs.append(_const_spec((1, D_MODEL)))
        args.append(final_norm)
    needs_slab = ple_args is not None or to_segment or to_natural
    scratch = [pltpu.VMEM((LANE_BLOCKS, TILE, LANES), F32)] if needs_slab else []
    return pl.pallas_call(
        functools.partial(_ffn_kernel, ple=ple_args is not None, final=final_norm is not None,
                          to_segment=to_segment, to_natural=to_natural),
        grid=(n_tiles,),
        in_specs=in_specs,
        out_specs=tok_spec,
        out_shape=jax.ShapeDtypeStruct((n_tok, D_MODEL), F32),
        scratch_shapes=scratch,
        compiler_params=pltpu.CompilerParams(
            dimension_semantics=("arbitrary",), vmem_limit_bytes=VMEM_LIMIT),
        name="ffn_ple" if ple_args is not None else "ffn",
    )(*args)


QUAD_HALO = CONV_HALO // 4
QUADS = SEG // 4
QUAD_TAPS = (CONV_K + 3) // 4
N_STREAMS = 9


def _with_history(cur, carry_ref, halo):
    rolled = pltpu.roll(cur[SEG - halo:], 1, axis=1)
    sub = lax.broadcasted_iota(jnp.int32, rolled.shape, 1)
    hist = jnp.where(sub == 0, carry_ref[...], rolled)
    carry_ref[...] = rolled
    return hist


def _mix_kernel(h_ref, nrm_ref, win_ref, pw_ref, ps_ref, cw_ref, cb_ref, lg_ref, lb_ref,
                cwo_ref, wo_ref, o_ref, zp_buf, s_buf, y_buf, ab_buf, zp_carry, c_carry, g_buf):
    s = pl.program_id(1)

    @pl.when(s == 0)
    def _():
        zp_carry[...] = jnp.zeros(zp_carry.shape, F32)
        c_carry[...] = jnp.zeros(c_carry.shape, F32)

        def delay_tap(d):
            if d >= CONV_K:
                return jnp.zeros((SUBLANES, D_MODEL), F32)
            return jnp.broadcast_to(cw_ref[CONV_K - 1 - d:CONV_K - d, :], (SUBLANES, D_MODEL))

        g0 = [delay_tap(2 * e) for e in range(2 * QUAD_TAPS)]
        g1 = [delay_tap(2 * e + 1) for e in range(2 * QUAD_TAPS)]
        tables = []
        for half in (g0, g1, [a + b for a, b in zip(g0, g1)]):
            even, odd = half[0::2], half[1::2]
            tables += [even, odd, [a + b for a, b in zip(even, odd)]]
        for wi, table in enumerate(tables):
            for t in range(QUAD_TAPS):
                for l in range(LANE_BLOCKS):
                    g_buf[wi, l, t] = _lane_block(table[t], l)

    h = h_ref[...]
    u = _rms(h, nrm_ref[...]).astype(BF16)

    def zcol(i):
        return _dot(u, win_ref[:, i * D_MODEL:(i + 1) * D_MODEL])

    zp = zcol(0).reshape(SEG, SUBLANES, D_MODEL)
    zp_buf[0:POOL_HALO] = _with_history(zp, zp_carry, POOL_HALO)
    zp_buf[POOL_HALO:] = zp
    c = (zcol(1) * jax.nn.sigmoid(zcol(2))).reshape(SEG, SUBLANES, D_MODEL)
    c_hist = _with_history(c, c_carry, CONV_HALO)
    c_quads = jnp.concatenate([c_hist, c], axis=0).reshape(QUAD_HALO + QUADS, 4, SUBLANES, D_MODEL)
    c0, c1, c2, c3 = (c_quads[:, i] for i in range(4))
    s01, s23 = c0 + c1, c2 + c3
    streams = (c0, c2, c0 + c2, c1, c3, c1 + c3, s01, s23, s01 + s23)
    for si, stream in enumerate(streams):
        for l in range(LANE_BLOCKS):
            s_buf[si, l] = _lane_block(stream, l)

    shp = (SEG, SUBLANES, POOL_GROUP)
    pos = (s * TILE + lax.broadcasted_iota(jnp.int32, shp, 1) * SEG
           + lax.broadcasted_iota(jnp.int32, shp, 0))
    a_parts = []
    for g, w in enumerate(POOL_WINDOWS):
        zg = zp_buf[:, :, g * POOL_GROUP:(g + 1) * POOL_GROUP]
        ssum, span = zg, 1
        while span < w:
            ssum = ssum[span:] + ssum[:-span]
            span *= 2
        ssum = ssum[POOL_HALO - (w - 1):]
        count = jnp.minimum(pos + 1, w).astype(F32)
        pooled = ssum / count - zg[POOL_HALO:]
        a_parts.append(_dot(pooled.reshape(TILE, POOL_GROUP).astype(BF16), pw_ref[g]))
    a = jnp.concatenate(a_parts, axis=-1) * ps_ref[...]

    n = QUADS

    def conv_lane_block(l, carry):
        def filt(si, first, count, taps=QUAD_TAPS):
            acc = None
            for t in range(taps):
                v = s_buf[si, l, pl.ds(first + QUAD_HALO - t, count)] * g_buf[si, l, t][None]
                acc = v if acc is None else acc + v
            return acc

        a, b = filt(0, 0, n), filt(1, -1, n + 1)
        ab_buf[0, 0:n] = a + b[:n]
        ab_buf[1, 0:n] = filt(2, 0, n) - (a + b[1:])
        a, b = filt(3, -1, n + 1), filt(4, -2, n + 2, taps=QUAD_TAPS - 1)
        ab_buf[2] = a + b[:n + 1]
        ab_buf[3] = filt(5, -1, n + 1) - (a + b[1:])
        a, b = filt(6, 0, n), filt(7, -1, n + 1)
        c_even, c_rest = a + b[:n], a + b[1:]
        a_even, a_odd = ab_buf[0, 0:n], ab_buf[1, 0:n]
        b_even, b_odd = ab_buf[2], ab_buf[3]
        y_buf[l, :, 0] = a_even + b_odd[:n]
        y_buf[l, :, 1] = c_even - a_even - b_even[1:]
        y_buf[l, :, 2] = a_odd + b_even[1:]
        y_buf[l, :, 3] = (filt(8, 0, n) - c_rest) - a_odd - b_odd[1:]
        return carry

    lax.fori_loop(0, LANE_BLOCKS, conv_lane_block, 0)
    g_pool = zcol(3)
    g_conv = zcol(4)
    y = jnp.concatenate([y_buf[l] for l in range(LANE_BLOCKS)], axis=-1).reshape(TILE, D_MODEL)
    y = y + cb_ref[...]
    mu = jnp.mean(y, axis=-1, keepdims=True)
    yc = y - mu
    var = jnp.mean(yc * yc, axis=-1, keepdims=True)
    yn = yc * lax.rsqrt(var + LN_EPS) * lg_ref[...] + lb_ref[...]
    cm = _dot((yn * jax.nn.sigmoid(yn)).astype(BF16), cwo_ref[...])

    m = jax.nn.sigmoid(g_pool) * a + jax.nn.sigmoid(g_conv) * cm
    o_ref[...] = h + _dot(m.astype(BF16), wo_ref[...])


def _mix_call(h, seq, layer, nrm, win, pw, ps, cw, cb, lg, lb, cwo, wo):
    n_tok = h.shape[0]
    tiles_per_seq = seq // TILE
    tok_spec = pl.BlockSpec((TILE, D_MODEL), lambda b, s: (b * tiles_per_seq + s, 0))
    vec = _const_spec((1, D_MODEL), layer)
    mat = _const_spec((D_MODEL, D_MODEL), layer)
    in_specs = [tok_spec, vec, _const_spec((D_MODEL, win.shape[2]), layer),
                _const_spec((N_POOL_GROUPS, POOL_GROUP, POOL_GROUP), layer), vec,
                _const_spec((CONV_K, D_MODEL), layer), vec, vec, vec, mat, mat]
    vreg_rows = lambda n: pltpu.VMEM((n, SUBLANES, D_MODEL), F32)
    slabs = lambda k, n: pltpu.VMEM((k, LANE_BLOCKS, n, SUBLANES, LANES), F32)
    return pl.pallas_call(
        _mix_kernel,
        grid=(n_tok // seq, tiles_per_seq),
        in_specs=in_specs,
        out_specs=tok_spec,
        out_shape=jax.ShapeDtypeStruct(h.shape, F32),
        scratch_shapes=[vreg_rows(POOL_HALO + SEG),
                        slabs(N_STREAMS, QUAD_HALO + QUADS),
                        pltpu.VMEM((LANE_BLOCKS, QUADS, 4, SUBLANES, LANES), F32),
                        pltpu.VMEM((4, QUADS + 1, SUBLANES, LANES), F32),
                        vreg_rows(POOL_HALO), vreg_rows(CONV_HALO),
                        slabs(N_STREAMS, QUAD_TAPS)],
        compiler_params=pltpu.CompilerParams(
            dimension_semantics=("arbitrary", "arbitrary"), vmem_limit_bytes=VMEM_LIMIT),
        name="mixer",
    )(h, nrm, win, pw, ps, cw, cb, lg, lb, cwo, wo)


def kernel(x, p, ffn1_norm, ffn1_w_gate, ffn1_w_up, ffn1_w_down, mix_norm, w_in, pool_w, pool_scale, conv_dw_w, conv_dw_b, conv_ln_g, conv_ln_b, conv_w_out, w_out, ffn2_norm, ffn2_w_gate, ffn2_w_up, ffn2_w_down, ple_norm, ple_w_gate, ple_w_proj, final_norm):
    bsz, seq, d = x.shape
    depth = p.shape[0]
    n_tok = bsz * seq
    rows = lambda v: v.reshape(v.shape[0], 1, -1)
    bf = lambda w: w.astype(BF16)
    h = x.reshape(n_tok, d)
    p_all = p.reshape(depth * n_tok, PLE_DIM)
    ffn1 = (rows(ffn1_norm), bf(ffn1_w_gate), bf(ffn1_w_up), bf(ffn1_w_down))
    ffn2 = (rows(ffn2_norm), bf(ffn2_w_gate), bf(ffn2_w_up), bf(ffn2_w_down))
    mix = (rows(mix_norm), bf(w_in), bf(pool_w), rows(pool_scale), conv_dw_w, rows(conv_dw_b),
           rows(conv_ln_g), rows(conv_ln_b), bf(conv_w_out), bf(w_out))
    ple_args = (p_all, rows(ple_norm), bf(ple_w_gate), bf(ple_w_proj))
    for i in range(depth):
        last = i == depth - 1
        h = _ffn_call(h, i, *ffn1, to_segment=(i == 0))
        h = _mix_call(h, seq, i, *mix)
        h = _ffn_call(h, i, *ffn2, ple_args=ple_args,
                      final_norm=final_norm.reshape(1, -1) if last else None, to_natural=last)
    return h.reshape(bsz, seq, d)
```
